```python
import math
import jax, jax.numpy as jnp
from jax import lax
import numpy as np

D_MODEL = 1024
BATCH = 8
SEQ = 2048
DEPTH = 1
DEC_BATCH = 128
DEC_SEQ = 8
PAST_LEN = 16384
PAGE_SIZE = 128

D_MIX = D_MODEL
A_WIDTH = D_MIX // 2
A_HEAD = 64
A_HEADS = A_WIDTH // A_HEAD
A_DECAY_LORA = 64
A_AAA_LORA = 64
A_GATE_LORA = 128
A_SHIFT_W = 3 * A_WIDTH + A_DECAY_LORA + A_AAA_LORA + A_GATE_LORA
B_WIDTH = D_MIX - A_WIDTH
B_HEAD = 128
B_HEADS = B_WIDTH // B_HEAD
B_CONV = 4
B_QKV = 3 * B_WIDTH
B_CHUNK = 64
B_PROJ_W = B_QKV + B_WIDTH + 2 * B_HEADS
IN_W = A_SHIFT_W + B_PROJ_W
D_FF = 2816
FFN_CONV = 3
PLE_DIM = 256
DN_ALPHA = (2.0 * DEPTH) ** 0.25
DN_BETA = (8.0 * DEPTH) ** -0.25
LN_EPS = 1e-5
GN_EPS = 64e-5
RMS_EPS = 1e-6
L2_EPS = 1e-12

kernel_name = 'hybrid_rwkv7_gdn_convffn_step'

F32 = jnp.float32


def _split_at(t, sizes):
    idx = np.cumsum(sizes)[:-1].tolist()
    return jnp.split(t, idx, axis=-1)


def _heads(t, n_heads):
    return t.reshape(t.shape[:-1] + (n_heads, t.shape[-1] // n_heads))


def _layer_norm(x, g, b):
    xf = x.astype(F32)
    mu = xf.mean(-1, keepdims=True)
    var = jnp.square(xf - mu).mean(-1, keepdims=True)
    return ((xf - mu) * lax.rsqrt(var + LN_EPS) * g + b).astype(x.dtype)


def _rms_norm(x, g):
    xf = x.astype(F32)
    return (xf * lax.rsqrt(jnp.square(xf).mean(-1, keepdims=True) + RMS_EPS) * g).astype(x.dtype)


def _l2norm(x):
    xf = x.astype(F32)
    return xf * lax.rsqrt(jnp.square(xf).sum(-1, keepdims=True) + L2_EPS)


def _causal_dwconv(buf, x, w):
    K = w.shape[0]
    T = x.shape[1]
    full = jnp.concatenate([buf.astype(x.dtype), x], axis=1)
    y = sum(full[:, j:j + T] * w[j] for j in range(K))
    return y, full[:, full.shape[1] - (K - 1):]


def _rwkv7_mix(u, shift_buf, wkv0, mu, w0, w_w2, a0, w_a2, w_g2, k_k, k_a, r_k, gn_g, gn_b):
    bsz, T, _ = u.shape
    prev = jnp.concatenate([shift_buf[:, None, :].astype(u.dtype), u[:, :-1]], axis=1)
    xs = u + (prev - u) * mu
    r, k, v, wd, ad, gd = _split_at(xs, (A_WIDTH, A_WIDTH, A_WIDTH, A_DECAY_LORA, A_AAA_LORA, A_GATE_LORA))
    w = -jax.nn.softplus(-(w0 + jnp.tanh(wd) @ w_w2)) - 0.5
    decay = jnp.exp(-jnp.exp(w.astype(F32)))
    a = jax.nn.sigmoid(a0 + ad @ w_a2)
    g = jax.nn.sigmoid(gd) @ w_g2
    kk = _l2norm(_heads(k * k_k, A_HEADS))
    k = k * (1.0 + (a - 1.0) * k_a)
    r_h, k_h, v_h, a_h, w_h = (_heads(t, A_HEADS).astype(F32) for t in (r, k, v, a, decay))

    def step(S, inp):
        r_t, w_t, k_t, v_t, kk_t, a_t = inp
        sa = jnp.einsum('bhvk,bhk->bhv', S, -kk_t)
        S = S * w_t[:, :, None, :] + sa[..., None] * (kk_t * a_t)[:, :, None, :] + v_t[..., None] * k_t[:, :, None, :]
        return S, jnp.einsum('bhvk,bhk->bhv', S, r_t)

    seq = tuple(jnp.swapaxes(t, 0, 1) for t in (r_h, w_h, k_h, v_h, kk, a_h))
    S_fin, o = lax.scan(step, wkv0.astype(F32), seq)
    o = jnp.swapaxes(o, 0, 1)
    mean = o.mean(-1, keepdims=True)
    var = jnp.square(o - mean).mean(-1, keepdims=True)
    o = (o - mean) * lax.rsqrt(var + GN_EPS) * gn_g.reshape(A_HEADS, A_HEAD) + gn_b.reshape(A_HEADS, A_HEAD)
    bonus = (r_h * k_h * r_k.reshape(A_HEADS, A_HEAD)).sum(-1, keepdims=True) * v_h
    o = (o + bonus).reshape(bsz, T, A_WIDTH) * g
    return o.astype(u.dtype), u[:, -1].astype(shift_buf.dtype), S_fin.astype(wkv0.dtype)


def _chunk_gated_delta(q, k, v, beta, g, S0):
    bsz, T, H, Dk = q.shape
    Dv = v.shape[-1]
    C = min(B_CHUNK, T)
    n = -(-T // C)
    pad = n * C - T

    def prep(t):
        t = jnp.pad(t.astype(F32), [(0, 0), (0, pad)] + [(0, 0)] * (t.ndim - 2))
        t = t.reshape((bsz, n, C) + t.shape[2:])
        return jnp.moveaxis(jnp.swapaxes(t, 2, 3), 1, 0)

    q, k, v, beta, g = (prep(t) for t in (q, k, v, beta, g))
    G = jnp.cumsum(g, axis=-1)
    diff = G[..., :, None] - G[..., None, :]
    incl = jnp.tril(jnp.ones((C, C), bool))
    strict = jnp.tril(jnp.ones((C, C), bool), -1)
    dec_incl = jnp.exp(jnp.where(incl, diff, -jnp.inf))
    dec_strict = jnp.where(strict, dec_incl, 0.0)
    kb = k * beta[..., None]
    Lmat = jnp.einsum('nbhid,nbhjd->nbhij', kb, k) * dec_strict
    eye = jnp.eye(C, dtype=F32)
    rhs = jnp.concatenate([v * beta[..., None], kb * jnp.exp(G)[..., None]], axis=-1)
    sol = lax.linalg.triangular_solve(eye + Lmat, rhs, left_side=True, lower=True, unit_diagonal=True)
    u_val, w_k = sol[..., :Dv], sol[..., Dv:]
    attn_in = jnp.einsum('nbhid,nbhjd->nbhij', q, k) * dec_incl
    q_dec = q * jnp.exp(G)[..., None]
    k_dec = k * jnp.exp(G[..., -1:] - G)[..., None]
    g_last = jnp.exp(G[..., -1])

    def step(S, inp):
        u_c, w_c, a_c, q_c, k_c, gl = inp
        v_new = u_c - jnp.einsum('bhcd,bhdv->bhcv', w_c, S)
        o = jnp.einsum('bhcd,bhdv->bhcv', q_c, S) + jnp.einsum('bhij,bhjv->bhiv', a_c, v_new)
        S = S * gl[..., None, None] + jnp.einsum('bhcd,bhcv->bhdv', k_c, v_new)
        return S, o

    S_fin, o = lax.scan(step, S0.astype(F32), (u_val, w_k, attn_in, q_dec, k_dec, g_last))
    o = jnp.swapaxes(jnp.moveaxis(o, 0, 1), 2, 3).reshape(bsz, n * C, H, Dv)[:, :T]
    return o, S_fin


def _gated_delta_mix(proj, conv_buf, S0, conv_w, a_log, dt_bias, norm_g):
    bsz, T, _ = proj.shape
    qkv, z, b, a = _split_at(proj, (B_QKV, B_WIDTH, B_HEADS, B_HEADS))
    qkv_c, new_buf = _causal_dwconv(conv_buf, qkv, conv_w)
    qkv_c = jax.nn.silu(qkv_c)
    q, k, v = (_heads(t, B_HEADS) for t in jnp.split(qkv_c, 3, axis=-1))
    q = _l2norm(q) * (B_HEAD ** -0.5)
    k = _l2norm(k)
    beta = jax.nn.sigmoid(b.astype(F32))
    g = -jnp.exp(a_log) * jax.nn.softplus(a.astype(F32) + dt_bias)
    o, S_fin = _chunk_gated_delta(q, k, v, beta, g, S0)
    o = _rms_norm(o, norm_g) * jax.nn.silu(_heads(z, B_HEADS).astype(F32))
    return o.reshape(bsz, T, B_WIDTH).astype(proj.dtype), new_buf.astype(conv_buf.dtype), S_fin.astype(S0.dtype)


def _conv_ffn(x, buf, w_up, conv_w, conv_b, w_down):
    gate, up = jnp.split(x @ w_up, 2, axis=-1)
    gate_c, new_buf = _causal_dwconv(buf, gate, conv_w)
    y = (jax.nn.silu(gate_c + conv_b) * up) @ w_down
    return y, new_buf.astype(buf.dtype)


def _layer(x, p, st, lw):
    (w_in, a_mu, a_w0, a_w_w2, a_a0, a_w_a2, a_w_g2, a_k_k, a_k_a, a_r_k, a_gn_g, a_gn_b,
     b_conv_w, b_a_log, b_dt_bias, b_norm_g, w_o, ln1_g, ln1_b, w_up, f_conv_w, f_conv_b,
     w_down, ln2_g, ln2_b, w_ple, ple_g, w_ple_gate) = lw
    a_wkv, a_shift, b_ssm, b_conv, f_conv = st
    proj = x @ w_in
    o_a, a_shift_new, a_wkv_new = _rwkv7_mix(proj[..., :A_SHIFT_W], a_shift, a_wkv, a_mu, a_w0, a_w_w2,
                                             a_a0, a_w_a2, a_w_g2, a_k_k, a_k_a, a_r_k, a_gn_g, a_gn_b)
    o_b, b_conv_new, b_ssm_new = _gated_delta_mix(proj[..., A_SHIFT_W:], b_conv, b_ssm, b_conv_w,
                                                  b_a_log, b_dt_bias, b_norm_g)
    mix = jnp.concatenate([o_a, o_b], axis=-1) @ w_o
    x = _layer_norm(DN_ALPHA * x + mix, ln1_g, ln1_b)
    ffn, f_conv_new = _conv_ffn(x, f_conv, w_up, f_conv_w, f_conv_b, w_down)
    x = _layer_norm(DN_ALPHA * x + ffn, ln2_g, ln2_b)
    e = _rms_norm(p @ w_ple, ple_g)
    x = x + jax.nn.sigmoid(x @ w_ple_gate) * e
    return x, (a_wkv_new, a_shift_new, b_ssm_new, b_conv_new, f_conv_new)


def _trunk(x, p, states, weights):
    new = []
    for i in range(DEPTH):
        x, st = _layer(x, p[i], tuple(s[i] for s in states), tuple(w[i] for w in weights))
        new.append(st)
    stacked = tuple(jnp.stack([st[j] for st in new]) for j in range(len(states)))
    return x, stacked


def setup_inputs(seed: int = 0) -> dict:
    key = jax.random.key(seed)
    ks = iter(jax.random.split(key, 64))
    nrm = lambda shape, scale: scale * jax.random.normal(next(ks), shape, F32)
    uni = lambda shape, lo, hi: jax.random.uniform(next(ks), shape, F32, lo, hi)
    L = DEPTH
    dt = jnp.exp(uni((L, B_HEADS), math.log(1e-3), math.log(1e-1)))
    return {
        'x_prompt': nrm((BATCH, SEQ, D_MODEL), 1.0),
        'x_sample': nrm((DEC_BATCH, DEC_SEQ, D_MODEL), 1.0),
        'p_prompt': nrm((L, BATCH, SEQ, PLE_DIM), 1.0),
        'p_sample': nrm((L, DEC_BATCH, DEC_SEQ, PLE_DIM), 1.0),
        'state_a_wkv': nrm((L, DEC_BATCH, A_HEADS, A_HEAD, A_HEAD), 0.3),
        'state_a_shift': nrm((L, DEC_BATCH, A_SHIFT_W), 1.0),
        'state_b_ssm': nrm((L, DEC_BATCH, B_HEADS, B_HEAD, B_HEAD), 0.1),
        'state_b_conv': nrm((L, DEC_BATCH, B_CONV - 1, B_QKV), 1.0),
        'state_ffn_conv': nrm((L, DEC_BATCH, FFN_CONV - 1, D_FF), 1.0),
        'w_in': nrm((L, D_MODEL, IN_W), D_MODEL ** -0.5),
        'a_mu': uni((L, A_SHIFT_W), 0.0, 1.0),
        'a_w0': uni((L, A_WIDTH), -6.0, -1.0),
        'a_w_w2': nrm((L, A_DECAY_LORA, A_WIDTH), 0.1),
        'a_a0': nrm((L, A_WIDTH), 0.1),
        'a_w_a2': nrm((L, A_AAA_LORA, A_WIDTH), 0.5 * A_AAA_LORA ** -0.5),
        'a_w_g2': nrm((L, A_GATE_LORA, A_WIDTH), A_GATE_LORA ** -0.5),
        'a_k_k': 0.85 + nrm((L, A_WIDTH), 0.02),
        'a_k_a': 1.0 + nrm((L, A_WIDTH), 0.02),
        'a_r_k': nrm((L, A_WIDTH), 0.1),
        'a_gn_g': 1.0 + nrm((L, A_WIDTH), 0.02),
        'a_gn_b': nrm((L, A_WIDTH), 0.02),
        'b_conv_w': nrm((L, B_CONV, B_QKV), 0.5),
        'b_a_log': jnp.log(uni((L, B_HEADS), 1.0, 16.0)),
        'b_dt_bias': dt + jnp.log(-jnp.expm1(-dt)),
        'b_norm_g': 1.0 + nrm((L, B_HEAD), 0.02),
        'w_o': nrm((L, D_MIX, D_MODEL), DN_BETA * D_MIX ** -0.5),
        'ln1_g': 1.0 + nrm((L, D_MODEL), 0.02),
        'ln1_b': nrm((L, D_MODEL), 0.02),
        'w_up': nrm((L, D_MODEL, 2 * D_FF), D_MODEL ** -0.5),
        'f_conv_w': nrm((L, FFN_CONV, D_FF), FFN_CONV ** -0.5),
        'f_conv_b': nrm((L, D_FF), 0.02),
        'w_down': nrm((L, D_FF, D_MODEL), DN_BETA * D_FF ** -0.5),
        'ln2_g': 1.0 + nrm((L, D_MODEL), 0.02),
        'ln2_b': nrm((L, D_MODEL), 0.02),
        'w_ple': nrm((L, PLE_DIM, D_MODEL), PLE_DIM ** -0.5),
        'ple_g': 1.0 + nrm((L, D_MODEL), 0.02),
        'w_ple_gate': nrm((L, D_MODEL, D_MODEL), D_MODEL ** -0.5),
    }


def reference(x_prompt, x_sample, p_prompt, p_sample, state_a_wkv, state_a_shift, state_b_ssm,
              state_b_conv, state_ffn_conv, w_in, a_mu, a_w0, a_w_w2, a_a0, a_w_a2, a_w_g2, a_k_k,
              a_k_a, a_r_k, a_gn_g, a_gn_b, b_conv_w, b_a_log, b_dt_bias, b_norm_g, w_o, ln1_g,
              ln1_b, w_up, f_conv_w, f_conv_b, w_down, ln2_g, ln2_b, w_ple, ple_g, w_ple_gate):
    weights = (w_in, a_mu, a_w0, a_w_w2, a_a0, a_w_a2, a_w_g2, a_k_k, a_k_a, a_r_k, a_gn_g, a_gn_b,
               b_conv_w, b_a_log, b_dt_bias, b_norm_g, w_o, ln1_g, ln1_b, w_up, f_conv_w, f_conv_b,
               w_down, ln2_g, ln2_b, w_ple, ple_g, w_ple_gate)
    bp = x_prompt.shape[0]
    zeros = lambda *s: jnp.zeros((DEPTH, bp) + s, x_prompt.dtype)
    prompt_init = (zeros(A_HEADS, A_HEAD, A_HEAD), zeros(A_SHIFT_W), zeros(B_HEADS, B_HEAD, B_HEAD),
                   zeros(B_CONV - 1, B_QKV), zeros(FFN_CONV - 1, D_FF))
    y_prompt, (pa_wkv, pa_shift, pb_ssm, pb_conv, pf_conv) = _trunk(x_prompt, p_prompt, prompt_init, weights)
    sample_init = (state_a_wkv, state_a_shift, state_b_ssm, state_b_conv, state_ffn_conv)
    y_sample, (sa_wkv, sa_shift, sb_ssm, sb_conv, sf_conv) = _trunk(x_sample, p_sample, sample_init, weights)
    return (y_prompt, y_sample, pa_wkv, pa_shift, pb_ssm, pb_conv, pf_conv,
            sa_wkv, sa_shift, sb_ssm, sb_conv, sf_conv)
```

```python
import functools
import math

import jax
import jax.numpy as jnp
from jax import lax
from jax.experimental import pallas as pl
from jax.experimental.pallas import tpu as pltpu

F32 = jnp.float32
BF16 = jnp.bfloat16

LANES = 128
SUBLANES = 8
VMEM_LIMIT_BYTES = 56 * 1024 * 1024

D_MODEL = 1024
A_WIDTH = 512
A_HEAD = 64
A_HEADS = 8
A_SHIFT_W = 1792
B_WIDTH = 512
B_HEAD = 128
B_HEADS = 4
B_CONV = 4
B_QKV = 1536
B_PROJ_PAD = 2176
D_FF = 2816
FFN_CONV = 3
PLE_DIM = 256
DEPTH = 1
DN_ALPHA = (2.0 * DEPTH) ** 0.25
LN_EPS = 1e-5
GN_EPS = 64e-5
RMS_EPS = 1e-6
L2_EPS = 1e-12

MIX_ROWS = 64
POST_ROWS = 256
PROJ_ROWS = 512

_NN = (((1,), (0,)), ((), ()))
_NT = (((1,), (1,)), ((), ()))
_TN = (((0,), (0,)), ((), ()))


def _dot(a, b, dims=_NN):
    return lax.dot_general(a, b, dims, preferred_element_type=F32)


def _split(x, n):
    parts = []
    r = x
    for i in range(n):
        h = r.astype(BF16)
        parts.append(h)
        if i + 1 < n:
            r = r - h.astype(F32)
    return parts


def _mm3(a, b, dims=_NN):
    ah, al = _split(a, 2)
    bh, bl = _split(b, 2)
    return _dot(ah, bh, dims) + (_dot(ah, bl, dims) + _dot(al, bh, dims))


def _mm_exact_lhs(m, x, n):
    out = None
    for part in _split(x, n):
        d = _dot(m, part)
        out = d if out is None else out + d
    return out


def _mm_exact_rhs(x, m, n):
    out = None
    for part in _split(x, n):
        d = _dot(part, m)
        out = d if out is None else out + d
    return out


def _sigmoid(x):
    return 1.0 / (1.0 + jnp.exp(-x))


def _silu(x):
    return x * _sigmoid(x)


def _softplus(x):
    return jnp.maximum(x, 0.0) + jnp.log1p(jnp.exp(-jnp.abs(x)))


def _iota(shape, dim):
    return lax.broadcasted_iota(jnp.int32, shape, dim)


def _shift_rows(x, j, prev, nb):
    rows = x.shape[0]
    rolled = pltpu.roll(x, j, 0)
    if nb == 1:
        t = _iota((SUBLANES, 1), 0)
        head = jnp.where(t >= j, rolled[0:SUBLANES], pltpu.roll(prev, j, 0))
        return jnp.concatenate([head, rolled[SUBLANES:]], axis=0)
    t = jnp.bitwise_and(_iota((rows, 1), 0), SUBLANES - 1)
    return jnp.where(t >= j, rolled, pltpu.roll(prev, rows - SUBLANES + j, 0))


def _load_prev(prev_scr, buf_ref, nb, k_prev):
    prev_scr[...] = jnp.zeros(prev_scr.shape, F32)
    for b in range(nb):
        prev_scr[SUBLANES * (b + 1) - k_prev:SUBLANES * (b + 1), :] = buf_ref[b]


def _block_masks(n, log_c):
    row = _iota((n, n), 0)
    col = _iota((n, n), 1)
    blk = jnp.right_shift(row, log_c) == jnp.right_shift(col, log_c)
    strict = jnp.logical_and(blk, row > col)
    incl = jnp.logical_and(blk, row >= col)
    return strict, incl, row == col


def _unit_lower_inverse(nmat, eye_f, log_c):
    t = eye_f + nmat
    p = nmat
    for _ in range(log_c - 1):
        p = _mm3(p, p)
        t = t + _mm3(t, p)
    return t


def _proj_kernel(x_ref, wa_ref, wb_ref, pa_ref, pb_ref):
    xb = x_ref[...].astype(BF16)
    pa_ref[...] = _dot(xb, wa_ref[...])
    pb_ref[...] = _dot(xb, wb_ref[...])


def _const_spec(shape):
    zeros = (0,) * len(shape)
    return pl.BlockSpec(shape, lambda *_: zeros, pipeline_mode=pl.Buffered(1))


def _proj_call(x2, w_a, w_b):
    m = x2.shape[0]
    tm = PROJ_ROWS
    return pl.pallas_call(
        _proj_kernel,
        grid=(m // tm,),
        in_specs=[pl.BlockSpec((tm, D_MODEL), lambda i: (i, 0)),
                  _const_spec(w_a.shape), _const_spec(w_b.shape)],
        out_specs=[pl.BlockSpec((tm, A_SHIFT_W), lambda i: (i, 0)),
                   pl.BlockSpec((tm, B_PROJ_PAD), lambda i: (i, 0))],
        out_shape=[jax.ShapeDtypeStruct((m, A_SHIFT_W), F32),
                   jax.ShapeDtypeStruct((m, B_PROJ_PAD), F32)],
        compiler_params=pltpu.CompilerParams(
            dimension_semantics=("arbitrary",), vmem_limit_bytes=VMEM_LIMIT_BYTES),
        name="in_proj",
    )(x2, w_a, w_b)


def _rwkv_kernel(pa_ref, sh0_ref, s0_ref, mu_ref, vec_ref, ww2_ref, wa2_ref, wg2_ref,
                 o_ref, sfin_ref, shout_ref, s_scr, prev_scr, *, nb, log_c, n_t):
    c = 1 << log_c
    rows = nb * c
    pairs = A_HEADS // 2
    t_idx = pl.program_id(1)

    @pl.when(t_idx == 0)
    def _init():
        z = jnp.zeros((A_HEAD, A_HEAD), F32)
        for b in range(nb):
            for p in range(pairs):
                top = jnp.concatenate([s0_ref[b, 2 * p], z], axis=1)
                bot = jnp.concatenate([z, s0_ref[b, 2 * p + 1]], axis=1)
                s_scr[b, p] = jnp.concatenate([top, bot], axis=0)
        _load_prev(prev_scr, sh0_ref, nb, 1)

    u = pa_ref[...]
    xs = u + (_shift_rows(u, 1, prev_scr[...], nb) - u) * mu_ref[...]
    if nb == 1:
        prev_scr[...] = u[rows - SUBLANES:rows, :]
        shout_ref[0] = u[rows - 1:rows, :]
    else:
        for b in range(nb):
            shout_ref[b] = u[c * (b + 1) - 1:c * (b + 1), :]

    w0 = vec_ref[0:1, :]
    a0 = vec_ref[1:2, :]
    k_k = vec_ref[2:3, :]
    k_a = vec_ref[3:4, :]
    r_k = vec_ref[4:5, :]
    gn_g = vec_ref[5:6, :]
    gn_b = vec_ref[6:7, :]

    r = xs[:, 0:512]
    k = xs[:, 512:1024]
    v = xs[:, 1024:1536]
    wa = xs[:, 1536:1664]
    gd = xs[:, 1664:1792]

    wlog = -_softplus(-(w0 + _dot(jnp.tanh(wa).astype(BF16), ww2_ref[...]))) - 0.5
    lw = -jnp.exp(wlog)
    a = _sigmoid(a0 + _dot(wa.astype(BF16), wa2_ref[...]))
    g = _dot(_sigmoid(gd).astype(BF16), wg2_ref[...])

    ones_head = jnp.where(
        jnp.right_shift(_iota((LANES, LANES), 0), 6) == jnp.right_shift(_iota((LANES, LANES), 1), 6),
        1.0, 0.0).astype(BF16)

    def head_sum(x):
        return jnp.concatenate(
            [_mm_exact_rhs(x[:, s * LANES:(s + 1) * LANES], ones_head, 2) for s in range(pairs)], axis=1)

    kkr = k * k_k
    kk = kkr * lax.rsqrt(head_sum(kkr * kkr) + L2_EPS)
    k2 = k * (1.0 + (a - 1.0) * k_a)
    bonus = head_sum(r * k2 * r_k) * v

    tri_r = _iota((rows, rows), 0)
    tri_c = _iota((rows, rows), 1)
    tri = jnp.where(jnp.logical_and(jnp.right_shift(tri_r, log_c) == jnp.right_shift(tri_c, log_c),
                                    tri_r >= tri_c), 1.0, 0.0).astype(BF16)
    gcum = _mm_exact_lhs(tri, lw, 3)
    e_g = jnp.exp(gcum)
    e_gi = jnp.exp(-gcum)
    a_t = -kk * jnp.exp(gcum - lw)
    b_t = kk * a * e_gi
    k_t = k2 * e_gi
    r_t = r * e_g

    n2 = 2 * rows
    strict, incl, eye = _block_masks(n2, log_c)
    eye_f = jnp.where(eye, 1.0, 0.0)
    lane0 = _iota((1, LANES), 1) < A_HEAD

    def stack(xp):
        return jnp.concatenate([jnp.where(lane0, xp, 0.0), jnp.where(lane0, 0.0, xp)], axis=0)

    def seq_rows(x, b):
        if nb == 1:
            return x
        return jnp.concatenate([x[b * c:(b + 1) * c], x[rows + b * c:rows + (b + 1) * c]], axis=0)

    def restack(parts):
        if nb == 1:
            return parts[0]
        return jnp.concatenate([q[0:c] for q in parts] + [q[c:2 * c] for q in parts], axis=0)

    o_parts = []
    for p in range(pairs):
        sl = slice(p * LANES, (p + 1) * LANES)
        a_s, b_s, k_s, r_s, v_s = (stack(x[:, sl]) for x in (a_t, b_t, k_t, r_t, v))
        nmat = jnp.where(strict, _mm3(a_s, b_s, _NT), 0.0)
        ak = jnp.where(strict, _mm3(a_s, k_s, _NT), 0.0)
        rb = jnp.where(incl, _mm3(r_s, b_s, _NT), 0.0)
        rk = jnp.where(incl, _mm3(r_s, k_s, _NT), 0.0)
        tinv = _unit_lower_inverse(nmat, eye_f, log_c)
        w_s = _mm3(tinv, a_s)
        u_s = _mm3(tinv, _mm3(ak, v_s))
        sa_parts, rs_parts = [], []
        for b in range(nb):
            s_old = s_scr[b, p]
            sa_b = _mm3(seq_rows(w_s, b), s_old, _NT) + seq_rows(u_s, b)
            rs_parts.append(_mm3(seq_rows(r_s, b), s_old, _NT))
            sa_parts.append(sa_b)
            upd = _mm3(sa_b, seq_rows(b_s, b), _TN) + _mm3(seq_rows(v_s, b), seq_rows(k_s, b), _TN)
            s_scr[b, p] = (s_old + upd) * e_g[(b + 1) * c - 1:(b + 1) * c, sl]
        sa_s = restack(sa_parts)
        o_s = restack(rs_parts) + _mm3(rb, sa_s) + _mm3(rk, v_s)
        o_parts.append(o_s[0:rows] + o_s[rows:n2])
    o = jnp.concatenate(o_parts, axis=1)

    mean = head_sum(o) * (1.0 / A_HEAD)
    d = o - mean
    var = head_sum(d * d) * (1.0 / A_HEAD)
    o_ref[...] = (d * lax.rsqrt(var + GN_EPS) * gn_g + gn_b + bonus) * g

    @pl.when(t_idx == n_t - 1)
    def _fin():
        for b in range(nb):
            for p in range(pairs):
                s_pair = s_scr[b, p]
                sfin_ref[b, 2 * p] = s_pair[0:A_HEAD, 0:A_HEAD]
                sfin_ref[b, 2 * p + 1] = s_pair[A_HEAD:LANES, A_HEAD:LANES]


def _rwkv_call(proj_a, shift0, wkv0, wts, nb, log_c, n_t):
    groups = shift0.shape[0] // nb
    rows = nb << log_c
    assert rows == MIX_ROWS
    prev_rows = SUBLANES if nb == 1 else rows
    row_map = lambda g, t: (g * n_t + t, 0)
    grp3 = lambda g, t: (g, 0, 0)
    grp4 = lambda g, t: (g, 0, 0, 0)
    kern = functools.partial(_rwkv_kernel, nb=nb, log_c=log_c, n_t=n_t)
    return pl.pallas_call(
        kern,
        grid=(groups, n_t),
        in_specs=[pl.BlockSpec((rows, A_SHIFT_W), row_map),
                  pl.BlockSpec((nb, 1, A_SHIFT_W), grp3),
                  pl.BlockSpec((nb, A_HEADS, A_HEAD, A_HEAD), grp4)]
                 + [_const_spec(w.shape) for w in wts],
        out_specs=[pl.BlockSpec((rows, A_WIDTH), row_map),
                   pl.BlockSpec((nb, A_HEADS, A_HEAD, A_HEAD), grp4),
                   pl.BlockSpec((nb, 1, A_SHIFT_W), grp3)],
        out_shape=[jax.ShapeDtypeStruct((proj_a.shape[0], A_WIDTH), F32),
                   jax.ShapeDtypeStruct(wkv0.shape, F32),
                   jax.ShapeDtypeStruct(shift0.shape, F32)],
        scratch_shapes=[pltpu.VMEM((nb, A_HEADS // 2, LANES, LANES), F32),
                        pltpu.VMEM((prev_rows, A_SHIFT_W), F32)],
        compiler_params=pltpu.CompilerParams(
            dimension_semantics=("arbitrary", "arbitrary"), vmem_limit_bytes=VMEM_LIMIT_BYTES),
        name="rwkv7_mix",
    )(proj_a, shift0, wkv0, *wts)


def _gdn_kernel(pb_ref, cb_ref, s0_ref, cw_ref, hv_ref,
                o_ref, sfin_ref, cout_ref, s_scr, prev_scr, *, nb, log_c, n_t):
    c = 1 << log_c
    rows = nb * c
    t_idx = pl.program_id(1)

    @pl.when(t_idx == 0)
    def _init():
        s_scr[...] = s0_ref[...]
        _load_prev(prev_scr, cb_ref, nb, B_CONV - 1)

    x = pb_ref[:, 0:B_QKV]
    z = pb_ref[:, B_QKV:B_QKV + B_WIDTH]
    ba = pb_ref[:, B_QKV + B_WIDTH:B_PROJ_PAD]
    prev = prev_scr[...]
    conv = (cw_ref[0:1, :] * _shift_rows(x, 3, prev, nb) + cw_ref[1:2, :] * _shift_rows(x, 2, prev, nb)
            + cw_ref[2:3, :] * _shift_rows(x, 1, prev, nb) + cw_ref[3:4, :] * x)
    if nb == 1:
        prev_scr[...] = x[rows - SUBLANES:rows, :]
        cout_ref[0] = x[rows - (B_CONV - 1):rows, :]
    else:
        for b in range(nb):
            cout_ref[b] = x[c * (b + 1) - (B_CONV - 1):c * (b + 1), :]
    qkv = _silu(conv)

    a_log = hv_ref[0:1, :]
    dt_bias = hv_ref[1:2, :]
    norm_g = hv_ref[2:3, :]
    beta_t = _sigmoid(ba)
    g_t = -jnp.exp(a_log) * _softplus(ba + dt_bias)

    tri_r = _iota((rows, rows), 0)
    tri_c = _iota((rows, rows), 1)
    tri = jnp.where(jnp.logical_and(jnp.right_shift(tri_r, log_c) == jnp.right_shift(tri_c, log_c),
                                    tri_r >= tri_c), 1.0, 0.0).astype(BF16)
    gcum_t = _mm_exact_lhs(tri, g_t, 3)

    def l2n(xh):
        return xh * lax.rsqrt(jnp.sum(xh * xh, axis=-1, keepdims=True) + L2_EPS)

    n2 = 2 * rows
    strict, incl, eye = _block_masks(n2, log_c)
    eye_f = jnp.where(eye, 1.0, 0.0)
    row2 = _iota((n2, n2), 0)
    col2 = _iota((n2, n2), 1)
    last_sel = col2 == jnp.bitwise_or(row2, c - 1)

    o_heads = [None] * B_HEADS
    for p in range(B_HEADS // 2):
        hs = (2 * p, 2 * p + 1)
        q_s = jnp.concatenate([l2n(qkv[:, h * LANES:(h + 1) * LANES]) * (B_HEAD ** -0.5) for h in hs], axis=0)
        k_s = jnp.concatenate([l2n(qkv[:, B_WIDTH + h * LANES:B_WIDTH + (h + 1) * LANES]) for h in hs], axis=0)
        v_s = jnp.concatenate([qkv[:, 2 * B_WIDTH + h * LANES:2 * B_WIDTH + (h + 1) * LANES] for h in hs], axis=0)
        beta_s = jnp.concatenate([beta_t[:, h:h + 1] for h in hs], axis=0)
        g_s = jnp.concatenate([gcum_t[:, B_HEADS + h:B_HEADS + h + 1] for h in hs], axis=0)
        gi = jnp.broadcast_to(g_s, (n2, n2))
        gj = gi.T
        dec = jnp.where(incl, jnp.exp(gi - gj), 0.0)
        dec_strict = jnp.where(strict, dec, 0.0)
        kb = k_s * beta_s
        lmat = _mm3(kb, k_s, _NT) * dec_strict
        tinv = _unit_lower_inverse(-lmat, eye_f, log_c)
        e_gs = jnp.exp(g_s)
        u_val = _mm3(tinv, v_s * beta_s)
        w_k = _mm3(tinv, kb * e_gs)
        attn = _mm3(q_s, k_s, _NT) * dec
        q_dec = q_s * e_gs
        g_last = jnp.sum(jnp.where(last_sel, gj, 0.0), axis=-1, keepdims=True)
        k_dec = k_s * jnp.exp(g_last - g_s)
        gl = jnp.exp(g_last)
        vn_parts, qs_parts = [], []
        for hh in range(2):
            for b in range(nb):
                r0 = hh * rows + b * c
                s_old = s_scr[b, hs[hh]]
                v_new = u_val[r0:r0 + c] - _mm3(w_k[r0:r0 + c], s_old)
                qs_parts.append(_mm3(q_dec[r0:r0 + c], s_old))
                vn_parts.append(v_new)
                s_scr[b, hs[hh]] = s_old * gl[r0:r0 + 1, :] + _mm3(k_dec[r0:r0 + c], v_new, _TN)
        o_s = jnp.concatenate(qs_parts, axis=0) + _mm3(attn, jnp.concatenate(vn_parts, axis=0))
        o_heads[hs[0]] = o_s[0:rows]
        o_heads[hs[1]] = o_s[rows:n2]

    outs = []
    for h in range(B_HEADS):
        oh = o_heads[h]
        on = oh * lax.rsqrt(jnp.mean(oh * oh, axis=-1, keepdims=True) + RMS_EPS) * norm_g
        outs.append(on * _silu(z[:, h * LANES:(h + 1) * LANES]))
    o_ref[...] = jnp.concatenate(outs, axis=1)

    @pl.when(t_idx == n_t - 1)
    def _fin():
        sfin_ref[...] = s_scr[...]


def _gdn_call(proj_b, conv0, ssm0, wts, nb, log_c, n_t):
    groups = conv0.shape[0] // nb
    rows = nb << log_c
    assert rows == MIX_ROWS
    prev_rows = SUBLANES if nb == 1 else rows
    row_map = lambda g, t: (g * n_t + t, 0)
    grp3 = lambda g, t: (g, 0, 0)
    grp4 = lambda g, t: (g, 0, 0, 0)
    kern = functools.partial(_gdn_kernel, nb=nb, log_c=log_c, n_t=n_t)
    return pl.pallas_call(
        kern,
        grid=(groups, n_t),
        in_specs=[pl.BlockSpec((rows, B_PROJ_PAD), row_map),
                  pl.BlockSpec((nb, B_CONV - 1, B_QKV), grp3),
                  pl.BlockSpec((nb, B_HEADS, B_HEAD, B_HEAD), grp4)]
                 + [_const_spec(w.shape) for w in wts],
        out_specs=[pl.BlockSpec((rows, B_WIDTH), row_map),
                   pl.BlockSpec((nb, B_HEADS, B_HEAD, B_HEAD), grp4),
                   pl.BlockSpec((nb, B_CONV - 1, B_QKV), grp3)],
        out_shape=[jax.ShapeDtypeStruct((proj_b.shape[0], B_WIDTH), F32),
                   jax.ShapeDtypeStruct(ssm0.shape, F32),
                   jax.ShapeDtypeStruct(conv0.shape, F32)],
        scratch_shapes=[pltpu.VMEM((nb, B_HEADS, B_HEAD, B_HEAD), F32),
                        pltpu.VMEM((prev_rows, B_QKV), F32)],
        compiler_params=pltpu.CompilerParams(
            dimension_semantics=("arbitrary", "arbitrary"), vmem_limit_bytes=VMEM_LIMIT_BYTES),
        name="gdn_mix",
    )(proj_b, conv0, ssm0, *wts)


def _layer_norm(x, g, b):
    mu = jnp.mean(x, axis=-1, keepdims=True)
    d = x - mu
    var = jnp.mean(d * d, axis=-1, keepdims=True)
    return d * lax.rsqrt(var + LN_EPS) * g + b


def _post_kernel(oa_ref, ob_ref, x_ref, p_ref, fb_ref, woa_ref, wob_ref, lnv_ref, wg_ref, wu_ref,
                 fv_ref, wd_ref, wple_ref, wpg_ref, y_ref, fout_ref, prev_scr, *, nb, n_t):
    rows = x_ref.shape[0]
    t_idx = pl.program_id(1)

    @pl.when(t_idx == 0)
    def _init():
        _load_prev(prev_scr, fb_ref, nb, FFN_CONV - 1)

    ln1_g, ln1_b = lnv_ref[0:1, :], lnv_ref[1:2, :]
    ln2_g, ln2_b = lnv_ref[2:3, :], lnv_ref[3:4, :]
    ple_g = lnv_ref[4:5, :]

    mix = _dot(oa_ref[...].astype(BF16), woa_ref[...]) + _dot(ob_ref[...].astype(BF16), wob_ref[...])
    h = _layer_norm(DN_ALPHA * x_ref[...] + mix, ln1_g, ln1_b)
    hb = h.astype(BF16)
    gate = _dot(hb, wg_ref[...])
    up = _dot(hb, wu_ref[...])
    prev = prev_scr[...]
    gate_c = (fv_ref[0:1, :] * _shift_rows(gate, 2, prev, nb) + fv_ref[1:2, :] * _shift_rows(gate, 1, prev, nb)
              + fv_ref[2:3, :] * gate)
    if nb == 1:
        prev_scr[...] = gate[rows - SUBLANES:rows, :]
        fout_ref[0] = gate[rows - (FFN_CONV - 1):rows, :]
    else:
        for b in range(nb):
            fout_ref[b] = gate[SUBLANES * (b + 1) - (FFN_CONV - 1):SUBLANES * (b + 1), :]
    act = _silu(gate_c + fv_ref[3:4, :]) * up
    ffn = _dot(act.astype(BF16), wd_ref[...])
    h2 = _layer_norm(DN_ALPHA * h + ffn, ln2_g, ln2_b)
    pe = _dot(p_ref[...].astype(BF16), wple_ref[...])
    e = pe * lax.rsqrt(jnp.mean(pe * pe, axis=-1, keepdims=True) + RMS_EPS) * ple_g
    y_ref[...] = h2 + _sigmoid(_dot(h2.astype(BF16), wpg_ref[...])) * e


def _post_call(o_a, o_b, x2, p2, fconv0, wts, nb, n_t):
    groups = fconv0.shape[0] // nb
    rows = POST_ROWS
    prev_rows = SUBLANES if nb == 1 else rows
    row_map = lambda g, t: (g * n_t + t, 0)
    grp3 = lambda g, t: (g, 0, 0)
    kern = functools.partial(_post_kernel, nb=nb, n_t=n_t)
    return pl.pallas_call(
        kern,
        grid=(groups, n_t),
        in_specs=[pl.BlockSpec((rows, A_WIDTH), row_map),
                  pl.BlockSpec((rows, B_WIDTH), row_map),
                  pl.BlockSpec((rows, D_MODEL), row_map),
                  pl.BlockSpec((rows, PLE_DIM), row_map),
                  pl.BlockSpec((nb, FFN_CONV - 1, D_FF), grp3)]
                 + [_const_spec(w.shape) for w in wts],
        out_specs=[pl.BlockSpec((rows, D_MODEL), row_map),
                   pl.BlockSpec((nb, FFN_CONV - 1, D_FF), grp3)],
        out_shape=[jax.ShapeDtypeStruct(x2.shape, F32),
                   jax.ShapeDtypeStruct(fconv0.shape, F32)],
        scratch_shapes=[pltpu.VMEM((prev_rows, D_FF), F32)],
        compiler_params=pltpu.CompilerParams(
            dimension_semantics=("arbitrary", "arbitrary"), vmem_limit_bytes=VMEM_LIMIT_BYTES),
        name="post_mix_ffn",
    )(o_a, o_b, x2, p2, fconv0, *wts)


def _pad_rows(w, n):
    return jnp.pad(w, ((0, n - w.shape[0]), (0, 0)))


def _pad_cols(w, n):
    return jnp.pad(w, ((0, 0), (0, n - w.shape[1])))


def _prepare_weights(w_in, a_mu, a_w0, a_w_w2, a_a0, a_w_a2, a_w_g2, a_k_k, a_k_a, a_r_k, a_gn_g,
                     a_gn_b, b_conv_w, b_a_log, b_dt_bias, b_norm_g, w_o, ln1_g, ln1_b, w_up,
                     f_conv_w, f_conv_b, w_down, ln2_g, ln2_b, w_ple, ple_g, w_ple_gate):
    w_in = w_in[0]
    lora = 64
    proj = (w_in[:, :A_SHIFT_W].astype(BF16), _pad_cols(w_in[:, A_SHIFT_W:], B_PROJ_PAD).astype(BF16))
    rwkv = (a_mu,
            _pad_rows(jnp.concatenate([a_w0, a_a0, a_k_k, a_k_a, a_r_k, a_gn_g, a_gn_b], axis=0), SUBLANES),
            _pad_rows(a_w_w2[0], LANES).astype(BF16),
            jnp.pad(a_w_a2[0], ((lora, LANES - 2 * lora), (0, 0))).astype(BF16),
            a_w_g2[0].astype(BF16))
    head_row = lambda vec: jnp.pad(vec, ((0, 0), (B_HEADS, LANES - 2 * B_HEADS)))
    gdn = (_pad_rows(b_conv_w[0], SUBLANES),
           _pad_rows(jnp.concatenate([head_row(b_a_log), head_row(b_dt_bias), b_norm_g], axis=0), SUBLANES))
    post = (w_o[0, :A_WIDTH].astype(BF16), w_o[0, A_WIDTH:].astype(BF16),
            _pad_rows(jnp.concatenate([ln1_g, ln1_b, ln2_g, ln2_b, ple_g], axis=0), SUBLANES),
            w_up[0, :, :D_FF].astype(BF16), w_up[0, :, D_FF:].astype(BF16),
            _pad_rows(jnp.concatenate([f_conv_w[0], f_conv_b], axis=0), SUBLANES),
            w_down[0].astype(BF16), w_ple[0].astype(BF16), w_ple_gate[0].astype(BF16))
    return proj, rwkv, gdn, post


def _trunk(x, p, states, weights):
    proj_w, rwkv_w, gdn_w, post_w = weights
    a_wkv, a_shift, b_ssm, b_conv, f_conv = (s[0] for s in states)
    bsz, seq, _ = x.shape
    x2 = x.reshape(bsz * seq, D_MODEL)
    p2 = p.reshape(bsz * seq, PLE_DIM)
    if seq % MIX_ROWS == 0:
        nb, log_c, n_t = 1, 6, seq // MIX_ROWS
        post_nb, post_nt = 1, seq // POST_ROWS
    else:
        assert seq == SUBLANES
        nb, log_c, n_t = MIX_ROWS // seq, 3, 1
        post_nb, post_nt = POST_ROWS // seq, 1
    proj_a, proj_b = _proj_call(x2, *proj_w)
    o_a, wkv_new, shift_new = _rwkv_call(proj_a, a_shift[:, None, :], a_wkv, rwkv_w, nb, log_c, n_t)
    o_b, ssm_new, conv_new = _gdn_call(proj_b, b_conv, b_ssm, gdn_w, nb, log_c, n_t)
    y2, fconv_new = _post_call(o_a, o_b, x2, p2, f_conv, post_w, post_nb, post_nt)
    new_states = (wkv_new, shift_new[:, 0, :], ssm_new, conv_new, fconv_new)
    return y2.reshape(bsz, seq, D_MODEL), tuple(s[None] for s in new_states)


def kernel(x_prompt, x_sample, p_prompt, p_sample, state_a_wkv, state_a_shift, state_b_ssm, state_b_conv, state_ffn_conv, w_in, a_mu, a_w0, a_w_w2, a_a0, a_w_a2, a_w_g2, a_k_k, a_k_a, a_r_k, a_gn_g, a_gn_b, b_conv_w, b_a_log, b_dt_bias, b_norm_g, w_o, ln1_g, ln1_b, w_up, f_conv_w, f_conv_b, w_down, ln2_g, ln2_b, w_ple, ple_g, w_ple_gate):
    assert w_in.shape[0] == DEPTH
    weights = _prepare_weights(w_in, a_mu, a_w0, a_w_w2, a_a0, a_w_a2, a_w_g2, a_k_k, a_k_a, a_r_k,
                               a_gn_g, a_gn_b, b_conv_w, b_a_log, b_dt_bias, b_norm_g, w_o, ln1_g,
                               ln1_b, w_up, f_conv_w, f_conv_b, w_down, ln2_g, ln2_b, w_ple, ple_g,
                               w_ple_gate)
    bp = x_prompt.shape[0]
    zeros = lambda *s: jnp.zeros((DEPTH, bp) + s, x_prompt.dtype)
    prompt_init = (zeros(A_HEADS, A_HEAD, A_HEAD), zeros(A_SHIFT_W), zeros(B_HEADS, B_HEAD, B_HEAD),
                   zeros(B_CONV - 1, B_QKV), zeros(FFN_CONV - 1, D_FF))
    y_prompt, prompt_states = _trunk(x_prompt, p_prompt[0], prompt_init, weights)
    sample_init = (state_a_wkv, state_a_shift, state_b_ssm, state_b_conv, state_ffn_conv)
    y_sample, sample_states = _trunk(x_sample, p_sample[0], sample_init, weights)
    return (y_prompt, y_sample) + prompt_states + sample_states
```

```python
import functools
import math

import jax
import jax.numpy as jnp
from jax import lax
from jax.experimental import pallas as pl
from jax.experimental.pallas import tpu as pltpu

F32 = jnp.float32
BF16 = jnp.bfloat16

LANES = 128
SUBLANES = 8
VMEM_LIMIT_BYTES = 56 * 1024 * 1024

D_MODEL = 1024
A_WIDTH = 512
A_HEAD = 64
A_HEADS = 8
A_SHIFT_W = 1792
B_WIDTH = 512
B_HEAD = 128
B_HEADS = 4
B_CONV = 4
B_QKV = 1536
B_PROJ_PAD = 2176
D_FF = 2816
FFN_CONV = 3
PLE_DIM = 256
DEPTH = 1
DN_ALPHA = (2.0 * DEPTH) ** 0.25
LN_EPS = 1e-5
GN_EPS = 64e-5
RMS_EPS = 1e-6
L2_EPS = 1e-12

MIX_ROWS = 64
POST_ROWS = 256
PROJ_ROWS = 512

_NN = (((1,), (0,)), ((), ()))
_NT = (((1,), (1,)), ((), ()))
_TN = (((0,), (0,)), ((), ()))


def _dot(a, b, dims=_NN):
    return lax.dot_general(a, b, dims, preferred_element_type=F32)


def _split(x, n):
    parts = []
    r = x
    for i in range(n):
        h = r.astype(BF16)
        parts.append(h)
        if i + 1 < n:
            r = r - h.astype(F32)
    return parts


def _mm(a, b, dims=_NN):
    return _dot(a.astype(BF16), b.astype(BF16), dims)


def _mm_exact_lhs(m, x, n):
    out = None
    for part in _split(x, n):
        d = _dot(m, part)
        out = d if out is None else out + d
    return out


def _mm_exact_rhs(x, m, n):
    out = None
    for part in _split(x, n):
        d = _dot(part, m)
        out = d if out is None else out + d
    return out


def _sigmoid(x):
    return 1.0 / (1.0 + jnp.exp(-x))


def _silu(x):
    return x * _sigmoid(x)


def _softplus(x):
    return jnp.maximum(x, 0.0) + jnp.log1p(jnp.exp(-jnp.abs(x)))


def _iota(shape, dim):
    return lax.broadcasted_iota(jnp.int32, shape, dim)


def _shift_rows(x, j, prev, nb):
    rows = x.shape[0]
    rolled = pltpu.roll(x, j, 0)
    if nb == 1:
        t = _iota((SUBLANES, 1), 0)
        head = jnp.where(t >= j, rolled[0:SUBLANES], pltpu.roll(prev, j, 0))
        return jnp.concatenate([head, rolled[SUBLANES:]], axis=0)
    t = jnp.bitwise_and(_iota((rows, 1), 0), SUBLANES - 1)
    return jnp.where(t >= j, rolled, pltpu.roll(prev, rows - SUBLANES + j, 0))


def _load_prev(prev_scr, buf_ref, nb, k_prev):
    prev_scr[...] = jnp.zeros(prev_scr.shape, F32)
    for b in range(nb):
        prev_scr[SUBLANES * (b + 1) - k_prev:SUBLANES * (b + 1), :] = buf_ref[b]


def _block_masks(n, log_c):
    row = _iota((n, n), 0)
    col = _iota((n, n), 1)
    blk = jnp.right_shift(row, log_c) == jnp.right_shift(col, log_c)
    strict = jnp.logical_and(blk, row > col)
    incl = jnp.logical_and(blk, row >= col)
    return strict, incl


def _unit_lower_inverse_minus_eye(nmat, log_c):
    tm = nmat
    p = nmat
    for _ in range(log_c - 1):
        p = _mm(p, p)
        tm = tm + p + _mm(tm, p)
    return tm


def _proj_kernel(x_ref, wa_ref, wb_ref, pa_ref, pb_ref):
    xb = x_ref[...].astype(BF16)
    pa_ref[...] = _dot(xb, wa_ref[...])
    pb_ref[...] = _dot(xb, wb_ref[...])


def _const_spec(shape):
    zeros = (0,) * len(shape)
    return pl.BlockSpec(shape, lambda *_: zeros, pipeline_mode=pl.Buffered(1))


def _proj_call(x2, w_a, w_b):
    m = x2.shape[0]
    tm = PROJ_ROWS
    return pl.pallas_call(
        _proj_kernel,
        grid=(m // tm,),
        in_specs=[pl.BlockSpec((tm, D_MODEL), lambda i: (i, 0)),
                  _const_spec(w_a.shape), _const_spec(w_b.shape)],
        out_specs=[pl.BlockSpec((tm, A_SHIFT_W), lambda i: (i, 0)),
                   pl.BlockSpec((tm, B_PROJ_PAD), lambda i: (i, 0))],
        out_shape=[jax.ShapeDtypeStruct((m, A_SHIFT_W), F32),
                   jax.ShapeDtypeStruct((m, B_PROJ_PAD), F32)],
        compiler_params=pltpu.CompilerParams(
            dimension_semantics=("arbitrary",), vmem_limit_bytes=VMEM_LIMIT_BYTES),
        name="in_proj",
    )(x2, w_a, w_b)


def _rwkv_kernel(pa_ref, sh0_ref, s0_ref, mu_ref, vec_ref, ww2_ref, wa2_ref, wg2_ref,
                 o_ref, sfin_ref, shout_ref, s_scr, prev_scr, *, nb, log_c, n_t):
    c = 1 << log_c
    rows = nb * c
    pairs = A_HEADS // 2
    t_idx = pl.program_id(1)

    @pl.when(t_idx == 0)
    def _init():
        z = jnp.zeros((A_HEAD, A_HEAD), F32)
        for b in range(nb):
            for p in range(pairs):
                top = jnp.concatenate([s0_ref[b, 2 * p], z], axis=1)
                bot = jnp.concatenate([z, s0_ref[b, 2 * p + 1]], axis=1)
                s_scr[b, p] = jnp.concatenate([top, bot], axis=0).T
        _load_prev(prev_scr, sh0_ref, nb, 1)

    u = pa_ref[...]
    xs = u + (_shift_rows(u, 1, prev_scr[...], nb) - u) * mu_ref[...]
    if nb == 1:
        prev_scr[...] = u[rows - SUBLANES:rows, :]
        shout_ref[0] = u[rows - 1:rows, :]
    else:
        for b in range(nb):
            shout_ref[b] = u[c * (b + 1) - 1:c * (b + 1), :]

    w0 = vec_ref[0:1, :]
    a0 = vec_ref[1:2, :]
    k_k = vec_ref[2:3, :]
    k_a = vec_ref[3:4, :]
    r_k = vec_ref[4:5, :]
    gn_g = vec_ref[5:6, :]
    gn_b = vec_ref[6:7, :]

    r = xs[:, 0:512]
    k = xs[:, 512:1024]
    v = xs[:, 1024:1536]
    wa = xs[:, 1536:1664]
    gd = xs[:, 1664:1792]

    wlog = -_softplus(-(w0 + _dot(jnp.tanh(wa).astype(BF16), ww2_ref[...]))) - 0.5
    lw = -jnp.exp(wlog)
    a = _sigmoid(a0 + _dot(wa.astype(BF16), wa2_ref[...]))
    g = _dot(_sigmoid(gd).astype(BF16), wg2_ref[...])

    ones_head = jnp.where(
        jnp.right_shift(_iota((LANES, LANES), 0), 6) == jnp.right_shift(_iota((LANES, LANES), 1), 6),
        1.0, 0.0).astype(BF16)

    def head_sum(x):
        return jnp.concatenate(
            [_mm_exact_rhs(x[:, s * LANES:(s + 1) * LANES], ones_head, 2) for s in range(pairs)], axis=1)

    kkr = k * k_k
    kk = kkr * lax.rsqrt(head_sum(kkr * kkr) + L2_EPS)
    k2 = k * (1.0 + (a - 1.0) * k_a)
    bonus = head_sum(r * k2 * r_k) * v

    tri_r = _iota((rows, rows), 0)
    tri_c = _iota((rows, rows), 1)
    tri = jnp.where(jnp.logical_and(jnp.right_shift(tri_r, log_c) == jnp.right_shift(tri_c, log_c),
                                    tri_r >= tri_c), 1.0, 0.0).astype(BF16)
    gcum = _mm_exact_lhs(tri, lw, 3)
    e_g = jnp.exp(gcum)
    e_gi = jnp.exp(-gcum)
    a_t = -kk * jnp.exp(gcum - lw)
    b_t = kk * a * e_gi
    k_t = k2 * e_gi
    r_t = r * e_g

    n2 = 2 * rows
    strict, incl = _block_masks(n2, log_c)
    lane0 = _iota((1, LANES), 1) < A_HEAD

    def stack(xp):
        return jnp.concatenate([jnp.where(lane0, xp, 0.0), jnp.where(lane0, 0.0, xp)], axis=0)

    def seq_rows(x, b):
        if nb == 1:
            return x
        return jnp.concatenate([x[b * c:(b + 1) * c], x[rows + b * c:rows + (b + 1) * c]], axis=0)

    def restack(parts):
        if nb == 1:
            return parts[0]
        return jnp.concatenate([q[0:c] for q in parts] + [q[c:2 * c] for q in parts], axis=0)

    col_seq = jnp.bitwise_and(jnp.right_shift(_iota((1, n2), 1), log_c), nb - 1)

    def seq_cols(xt, b):
        return xt if nb == 1 else jnp.where(col_seq == b, xt, 0.0)

    a_l, r_l, v_l, bt_l, kt_l, n_l, ak_l, rbk_l, dec_l = ([] for _ in range(9))
    for p in range(pairs):
        sl = slice(p * LANES, (p + 1) * LANES)
        a_s, b_s, k_s, r_s, v_s = (stack(x[:, sl]) for x in (a_t, b_t, k_t, r_t, v))
        bk_t = jnp.concatenate([b_s, k_s], axis=0).T
        x = _mm(jnp.concatenate([a_s, r_s], axis=0), bk_t)
        a_l.append(a_s)
        r_l.append(r_s.astype(BF16))
        v_l.append(v_s.astype(BF16))
        bt_l.append(bk_t[:, 0:n2])
        kt_l.append(bk_t[:, n2:2 * n2])
        n_l.append(jnp.where(strict, x[0:n2, 0:n2], 0.0))
        ak_l.append(jnp.where(strict, x[0:n2, n2:2 * n2], 0.0))
        rbk_l.append(jnp.concatenate([jnp.where(incl, x[n2:2 * n2, 0:n2], 0.0),
                                      jnp.where(incl, x[n2:2 * n2, n2:2 * n2], 0.0)], axis=1))
        dec_l.append(e_g[:, sl].T)
    tm_l = list(n_l)
    pw_l = list(n_l)
    for _ in range(log_c - 1):
        pw_l = [_mm(q, q) for q in pw_l]
        tm_l = [t + q + _mm(t, q) for t, q in zip(tm_l, pw_l)]
    akv_l = [_mm(ak, v_s) for ak, v_s in zip(ak_l, v_l)]
    wu_l = []
    for p in range(pairs):
        rhs = jnp.concatenate([a_l[p], akv_l[p]], axis=1)
        wu_l.append(rhs + _mm(tm_l[p], rhs))
    kv_l = [[_mm(seq_cols(kt_l[p], b), v_l[p]) for b in range(nb)] for p in range(pairs)]
    sa_l, rs_l = [], []
    for p in range(pairs):
        sa_parts, rs_parts = [], []
        for b in range(nb):
            m_old = s_scr[b, p]
            sa_parts.append(_mm(seq_rows(wu_l[p][:, 0:LANES], b), m_old)
                            + seq_rows(wu_l[p][:, LANES:2 * LANES], b))
            rs_parts.append(_mm(seq_rows(r_l[p], b), m_old))
        sa_l.append(restack(sa_parts))
        rs_l.append(restack(rs_parts))
    for p in range(pairs):
        for b in range(nb):
            upd = _mm(seq_cols(bt_l[p], b), sa_l[p]) + kv_l[p][b]
            s_scr[b, p] = (s_scr[b, p] + upd) * dec_l[p][:, (b + 1) * c - 1:(b + 1) * c]
    o_parts = []
    for p in range(pairs):
        o_s = rs_l[p] + _mm(rbk_l[p], jnp.concatenate([sa_l[p].astype(BF16), v_l[p]], axis=0))
        o_parts.append(o_s[0:rows] + o_s[rows:n2])
    o = jnp.concatenate(o_parts, axis=1)

    mean = head_sum(o) * (1.0 / A_HEAD)
    d = o - mean
    var = head_sum(d * d) * (1.0 / A_HEAD)
    o_ref[...] = (d * lax.rsqrt(var + GN_EPS) * gn_g + gn_b + bonus) * g

    @pl.when(t_idx == n_t - 1)
    def _fin():
        for b in range(nb):
            for p in range(pairs):
                s_pair = s_scr[b, p].T
                sfin_ref[b, 2 * p] = s_pair[0:A_HEAD, 0:A_HEAD]
                sfin_ref[b, 2 * p + 1] = s_pair[A_HEAD:LANES, A_HEAD:LANES]


def _rwkv_call(proj_a, shift0, wkv0, wts, nb, log_c, n_t):
    groups = shift0.shape[0] // nb
    rows = nb << log_c
    assert rows == MIX_ROWS
    prev_rows = SUBLANES if nb == 1 else rows
    row_map = lambda g, t: (g * n_t + t, 0)
    grp3 = lambda g, t: (g, 0, 0)
    grp4 = lambda g, t: (g, 0, 0, 0)
    kern = functools.partial(_rwkv_kernel, nb=nb, log_c=log_c, n_t=n_t)
    return pl.pallas_call(
        kern,
        grid=(groups, n_t),
        in_specs=[pl.BlockSpec((rows, A_SHIFT_W), row_map),
                  pl.BlockSpec((nb, 1, A_SHIFT_W), grp3),
                  pl.BlockSpec((nb, A_HEADS, A_HEAD, A_HEAD), grp4)]
                 + [_const_spec(w.shape) for w in wts],
        out_specs=[pl.BlockSpec((rows, A_WIDTH), row_map),
                   pl.BlockSpec((nb, A_HEADS, A_HEAD, A_HEAD), grp4),
                   pl.BlockSpec((nb, 1, A_SHIFT_W), grp3)],
        out_shape=[jax.ShapeDtypeStruct((proj_a.shape[0], A_WIDTH), F32),
                   jax.ShapeDtypeStruct(wkv0.shape, F32),
                   jax.ShapeDtypeStruct(shift0.shape, F32)],
        scratch_shapes=[pltpu.VMEM((nb, A_HEADS // 2, LANES, LANES), F32),
                        pltpu.VMEM((prev_rows, A_SHIFT_W), F32)],
        compiler_params=pltpu.CompilerParams(
            dimension_semantics=("arbitrary", "arbitrary"), vmem_limit_bytes=VMEM_LIMIT_BYTES),
        name="rwkv7_mix",
    )(proj_a, shift0, wkv0, *wts)


def _gdn_kernel(pb_ref, cb_ref, s0_ref, cw_ref, hv_ref,
                o_ref, sfin_ref, cout_ref, s_scr, prev_scr, *, nb, log_c, n_t):
    c = 1 << log_c
    rows = nb * c
    t_idx = pl.program_id(1)

    @pl.when(t_idx == 0)
    def _init():
        s_scr[...] = s0_ref[...]
        _load_prev(prev_scr, cb_ref, nb, B_CONV - 1)

    x = pb_ref[:, 0:B_QKV]
    z = pb_ref[:, B_QKV:B_QKV + B_WIDTH]
    ba = pb_ref[:, B_QKV + B_WIDTH:B_PROJ_PAD]
    prev = prev_scr[...]
    conv = (cw_ref[0:1, :] * _shift_rows(x, 3, prev, nb) + cw_ref[1:2, :] * _shift_rows(x, 2, prev, nb)
            + cw_ref[2:3, :] * _shift_rows(x, 1, prev, nb) + cw_ref[3:4, :] * x)
    if nb == 1:
        prev_scr[...] = x[rows - SUBLANES:rows, :]
        cout_ref[0] = x[rows - (B_CONV - 1):rows, :]
    else:
        for b in range(nb):
            cout_ref[b] = x[c * (b + 1) - (B_CONV - 1):c * (b + 1), :]
    qkv = _silu(conv)

    a_log = hv_ref[0:1, :]
    dt_bias = hv_ref[1:2, :]
    norm_g = hv_ref[2:3, :]
    beta_t = _sigmoid(ba)
    g_t = -jnp.exp(a_log) * _softplus(ba + dt_bias)

    tri_r = _iota((rows, rows), 0)
    tri_c = _iota((rows, rows), 1)
    tri = jnp.where(jnp.logical_and(jnp.right_shift(tri_r, log_c) == jnp.right_shift(tri_c, log_c),
                                    tri_r >= tri_c), 1.0, 0.0).astype(BF16)
    gcum_t = _mm_exact_lhs(tri, g_t, 3)

    def l2n(xh):
        return xh * lax.rsqrt(jnp.sum(xh * xh, axis=-1, keepdims=True) + L2_EPS)

    n2 = 2 * rows
    strict, incl = _block_masks(n2, log_c)
    row2 = _iota((n2, n2), 0)
    col2 = _iota((n2, n2), 1)
    last_sel_t = row2 == jnp.bitwise_or(col2, c - 1)
    col_unit = jnp.right_shift(_iota((1, n2), 1), log_c)
    units = 2 * nb
    npairs = B_HEADS // 2

    nl_l, rhs_l, attn_l, qdec_l, kdt_l, gl_l = ([] for _ in range(6))
    for p in range(npairs):
        hs = (2 * p, 2 * p + 1)
        q_s = jnp.concatenate([l2n(qkv[:, h * LANES:(h + 1) * LANES]) * (B_HEAD ** -0.5) for h in hs], axis=0)
        k_s = jnp.concatenate([l2n(qkv[:, B_WIDTH + h * LANES:B_WIDTH + (h + 1) * LANES]) for h in hs], axis=0)
        v_s = jnp.concatenate([qkv[:, 2 * B_WIDTH + h * LANES:2 * B_WIDTH + (h + 1) * LANES] for h in hs], axis=0)
        beta_s = jnp.concatenate([beta_t[:, h:h + 1] for h in hs], axis=0)
        g_s = jnp.concatenate([gcum_t[:, B_HEADS + h:B_HEADS + h + 1] for h in hs], axis=0)
        gi = jnp.broadcast_to(g_s, (n2, n2))
        gj = gi.T
        dec = jnp.where(incl, jnp.exp(gi - gj), 0.0)
        kb = k_s * beta_s
        ks_t = k_s.T
        x = _mm(jnp.concatenate([kb, q_s], axis=0), ks_t)
        e_gs = jnp.exp(g_s)
        g_last = jnp.sum(jnp.where(last_sel_t, gi, 0.0), axis=0, keepdims=True)
        nl_l.append(-(x[0:n2] * jnp.where(strict, dec, 0.0)))
        attn_l.append((x[n2:2 * n2] * dec).astype(BF16))
        rhs_l.append(jnp.concatenate([v_s * beta_s, kb * e_gs], axis=1))
        qdec_l.append(q_s * e_gs)
        kdt_l.append(ks_t * jnp.exp(g_last - gj[0:1, :]))
        gl_l.append(jnp.exp(g_last))
    tm_l = list(nl_l)
    pw_l = list(nl_l)
    for _ in range(log_c - 1):
        pw_l = [_mm(q, q) for q in pw_l]
        tm_l = [t + q + _mm(t, q) for t, q in zip(tm_l, pw_l)]
    uw_l = [rhs + _mm(tm, rhs) for tm, rhs in zip(tm_l, rhs_l)]
    vn_l, qs_l = [], []
    for p in range(npairs):
        vn_parts, qs_parts = [], []
        for u in range(units):
            s_old = s_scr[u % nb, 2 * p + u // nb]
            vn_parts.append(uw_l[p][u * c:(u + 1) * c, 0:LANES]
                            - _mm(uw_l[p][u * c:(u + 1) * c, LANES:2 * LANES], s_old))
            qs_parts.append(_mm(qdec_l[p][u * c:(u + 1) * c], s_old))
        vn_l.append(jnp.concatenate(vn_parts, axis=0).astype(BF16))
        qs_l.append(jnp.concatenate(qs_parts, axis=0))
    for p in range(npairs):
        for u in range(units):
            kd_u = jnp.where(col_unit == u, kdt_l[p], 0.0)
            s_scr[u % nb, 2 * p + u // nb] = (s_scr[u % nb, 2 * p + u // nb] * gl_l[p][:, u * c:u * c + 1]
                                              + _mm(kd_u, vn_l[p]))
    o_heads = []
    for p in range(npairs):
        o_s = qs_l[p] + _mm(attn_l[p], vn_l[p])
        o_heads += [o_s[0:rows], o_s[rows:n2]]

    outs = []
    for h in range(B_HEADS):
        oh = o_heads[h]
        on = oh * lax.rsqrt(jnp.mean(oh * oh, axis=-1, keepdims=True) + RMS_EPS) * norm_g
        outs.append(on * _silu(z[:, h * LANES:(h + 1) * LANES]))
    o_ref[...] = jnp.concatenate(outs, axis=1)

    @pl.when(t_idx == n_t - 1)
    def _fin():
        sfin_ref[...] = s_scr[...]


def _gdn_call(proj_b, conv0, ssm0, wts, nb, log_c, n_t):
    groups = conv0.shape[0] // nb
    rows = nb << log_c
    assert rows == MIX_ROWS
    prev_rows = SUBLANES if nb == 1 else rows
    row_map = lambda g, t: (g * n_t + t, 0)
    grp3 = lambda g, t: (g, 0, 0)
    grp4 = lambda g, t: (g, 0, 0, 0)
    kern = functools.partial(_gdn_kernel, nb=nb, log_c=log_c, n_t=n_t)
    return pl.pallas_call(
        kern,
        grid=(groups, n_t),
        in_specs=[pl.BlockSpec((rows, B_PROJ_PAD), row_map),
                  pl.BlockSpec((nb, B_CONV - 1, B_QKV), grp3),
                  pl.BlockSpec((nb, B_HEADS, B_HEAD, B_HEAD), grp4)]
                 + [_const_spec(w.shape) for w in wts],
        out_specs=[pl.BlockSpec((rows, B_WIDTH), row_map),
                   pl.BlockSpec((nb, B_HEADS, B_HEAD, B_HEAD), grp4),
                   pl.BlockSpec((nb, B_CONV - 1, B_QKV), grp3)],
        out_shape=[jax.ShapeDtypeStruct((proj_b.shape[0], B_WIDTH), F32),
                   jax.ShapeDtypeStruct(ssm0.shape, F32),
                   jax.ShapeDtypeStruct(conv0.shape, F32)],
        scratch_shapes=[pltpu.VMEM((nb, B_HEADS, B_HEAD, B_HEAD), F32),
                        pltpu.VMEM((prev_rows, B_QKV), F32)],
        compiler_params=pltpu.CompilerParams(
            dimension_semantics=("arbitrary", "arbitrary"), vmem_limit_bytes=VMEM_LIMIT_BYTES),
        name="gdn_mix",
    )(proj_b, conv0, ssm0, *wts)


def _layer_norm(x, g, b):
    mu = jnp.mean(x, axis=-1, keepdims=True)
    d = x - mu
    var = jnp.mean(d * d, axis=-1, keepdims=True)
    return d * lax.rsqrt(var + LN_EPS) * g + b


def _post_kernel(oa_ref, ob_ref, x_ref, p_ref, fb_ref, woa_ref, wob_ref, lnv_ref, wg_ref, wu_ref,
                 fv_ref, wd_ref, wple_ref, wpg_ref, y_ref, fout_ref, prev_scr, *, nb, n_t):
    rows = x_ref.shape[0]
    t_idx = pl.program_id(1)

    @pl.when(t_idx == 0)
    def _init():
        _load_prev(prev_scr, fb_ref, nb, FFN_CONV - 1)

    ln1_g, ln1_b = lnv_ref[0:1, :], lnv_ref[1:2, :]
    ln2_g, ln2_b = lnv_ref[2:3, :], lnv_ref[3:4, :]
    ple_g = lnv_ref[4:5, :]

    mix = _dot(oa_ref[...].astype(BF16), woa_ref[...]) + _dot(ob_ref[...].astype(BF16), wob_ref[...])
    h = _layer_norm(DN_ALPHA * x_ref[...] + mix, ln1_g, ln1_b)
    hb = h.astype(BF16)
    gate = _dot(hb, wg_ref[...])
    up = _dot(hb, wu_ref[...])
    prev = prev_scr[...]
    gate_c = (fv_ref[0:1, :] * _shift_rows(gate, 2, prev, nb) + fv_ref[1:2, :] * _shift_rows(gate, 1, prev, nb)
              + fv_ref[2:3, :] * gate)
    if nb == 1:
        prev_scr[...] = gate[rows - SUBLANES:rows, :]
        fout_ref[0] = gate[rows - (FFN_CONV - 1):rows, :]
    else:
        for b in range(nb):
            fout_ref[b] = gate[SUBLANES * (b + 1) - (FFN_CONV - 1):SUBLANES * (b + 1), :]
    act = _silu(gate_c + fv_ref[3:4, :]) * up
    ffn = _dot(act.astype(BF16), wd_ref[...])
    h2 = _layer_norm(DN_ALPHA * h + ffn, ln2_g, ln2_b)
    pe = _dot(p_ref[...].astype(BF16), wple_ref[...])
    e = pe * lax.rsqrt(jnp.mean(pe * pe, axis=-1, keepdims=True) + RMS_EPS) * ple_g
    y_ref[...] = h2 + _sigmoid(_dot(h2.astype(BF16), wpg_ref[...])) * e


def _post_call(o_a, o_b, x2, p2, fconv0, wts, nb, n_t):
    groups = fconv0.shape[0] // nb
    rows = POST_ROWS
    prev_rows = SUBLANES if nb == 1 else rows
    row_map = lambda g, t: (g * n_t + t, 0)
    grp3 = lambda g, t: (g, 0, 0)
    kern = functools.partial(_post_kernel, nb=nb, n_t=n_t)
    return pl.pallas_call(
        kern,
        grid=(groups, n_t),
        in_specs=[pl.BlockSpec((rows, A_WIDTH), row_map),
                  pl.BlockSpec((rows, B_WIDTH), row_map),
                  pl.BlockSpec((rows, D_MODEL), row_map),
                  pl.BlockSpec((rows, PLE_DIM), row_map),
                  pl.BlockSpec((nb, FFN_CONV - 1, D_FF), grp3)]
                 + [_const_spec(w.shape) for w in wts],
        out_specs=[pl.BlockSpec((rows, D_MODEL), row_map),
                   pl.BlockSpec((nb, FFN_CONV - 1, D_FF), grp3)],
        out_shape=[jax.ShapeDtypeStruct(x2.shape, F32),
                   jax.ShapeDtypeStruct(fconv0.shape, F32)],
        scratch_shapes=[pltpu.VMEM((prev_rows, D_FF), F32)],
        compiler_params=pltpu.CompilerParams(
            dimension_semantics=("arbitrary", "arbitrary"), vmem_limit_bytes=VMEM_LIMIT_BYTES),
        name="post_mix_ffn",
    )(o_a, o_b, x2, p2, fconv0, *wts)


def _pad_rows(w, n):
    return jnp.pad(w, ((0, n - w.shape[0]), (0, 0)))


def _pad_cols(w, n):
    return jnp.pad(w, ((0, 0), (0, n - w.shape[1])))


def _prepare_weights(w_in, a_mu, a_w0, a_w_w2, a_a0, a_w_a2, a_w_g2, a_k_k, a_k_a, a_r_k, a_gn_g,
                     a_gn_b, b_conv_w, b_a_log, b_dt_bias, b_norm_g, w_o, ln1_g, ln1_b, w_up,
                     f_conv_w, f_conv_b, w_down, ln2_g, ln2_b, w_ple, ple_g, w_ple_gate):
    w_in = w_in[0]
    lora = 64
    proj = (w_in[:, :A_SHIFT_W].astype(BF16), _pad_cols(w_in[:, A_SHIFT_W:], B_PROJ_PAD).astype(BF16))
    rwkv = (a_mu,
            _pad_rows(jnp.concatenate([a_w0, a_a0, a_k_k, a_k_a, a_r_k, a_gn_g, a_gn_b], axis=0), SUBLANES),
            _pad_rows(a_w_w2[0], LANES).astype(BF16),
            jnp.pad(a_w_a2[0], ((lora, LANES - 2 * lora), (0, 0))).astype(BF16),
            a_w_g2[0].astype(BF16))
    head_row = lambda vec: jnp.pad(vec, ((0, 0), (B_HEADS, LANES - 2 * B_HEADS)))
    gdn = (_pad_rows(b_conv_w[0], SUBLANES),
           _pad_rows(jnp.concatenate([head_row(b_a_log), head_row(b_dt_bias), b_norm_g], axis=0), SUBLANES))
    post = (w_o[0, :A_WIDTH].astype(BF16), w_o[0, A_WIDTH:].astype(BF16),
            _pad_rows(jnp.concatenate([ln1_g, ln1_b, ln2_g, ln2_b, ple_g], axis=0), SUBLANES),
            w_up[0, :, :D_FF].astype(BF16), w_up[0, :, D_FF:].astype(BF16),
            _pad_rows(jnp.concatenate([f_conv_w[0], f_conv_b], axis=0), SUBLANES),
            w_down[0].astype(BF16), w_ple[0].astype(BF16), w_ple_gate[0].astype(BF16))
    return proj, rwkv, gdn, post


def _trunk(x, p, states, weights):
    proj_w, rwkv_w, gdn_w, post_w = weights
    a_wkv, a_shift, b_ssm, b_conv, f_conv = (s[0] for s in states)
    bsz, seq, _ = x.shape
    x2 = x.reshape(bsz * seq, D_MODEL)
    p2 = p.reshape(bsz * seq, PLE_DIM)
    if seq % MIX_ROWS == 0:
        nb, log_c, n_t = 1, 6, seq // MIX_ROWS
        post_nb, post_nt = 1, seq // POST_ROWS
    else:
        assert seq == SUBLANES
        nb, log_c, n_t = MIX_ROWS // seq, 3, 1
        post_nb, post_nt = POST_ROWS // seq, 1
    proj_a, proj_b = _proj_call(x2, *proj_w)
    o_a, wkv_new, shift_new = _rwkv_call(proj_a, a_shift[:, None, :], a_wkv, rwkv_w, nb, log_c, n_t)
    o_b, ssm_new, conv_new = _gdn_call(proj_b, b_conv, b_ssm, gdn_w, nb, log_c, n_t)
    y2, fconv_new = _post_call(o_a, o_b, x2, p2, f_conv, post_w, post_nb, post_nt)
    new_states = (wkv_new, shift_new[:, 0, :], ssm_new, conv_new, fconv_new)
    return y2.reshape(bsz, seq, D_MODEL), tuple(s[None] for s in new_states)


def kernel(x_prompt, x_sample, p_prompt, p_sample, state_a_wkv, state_a_shift, state_b_ssm, state_b_conv, state_ffn_conv, w_in, a_mu, a_w0, a_w_w2, a_a0, a_w_a2, a_w_g2, a_k_k, a_k_a, a_r_k, a_gn_g, a_gn_b, b_conv_w, b_a_log, b_dt_bias, b_norm_g, w_o, ln1_g, ln1_b, w_up, f_conv_w, f_conv_b, w_down, ln2_g, ln2_b, w_ple, ple_g, w_ple_gate):
    assert w_in.shape[0] == DEPTH
    weights = _prepare_weights(w_in, a_mu, a_w0, a_w_w2, a_a0, a_w_a2, a_w_g2, a_k_k, a_k_a, a_r_k,
                               a_gn_g, a_gn_b, b_conv_w, b_a_log, b_dt_bias, b_norm_g, w_o, ln1_g,
                               ln1_b, w_up, f_conv_w, f_conv_b, w_down, ln2_g, ln2_b, w_ple, ple_g,
                               w_ple_gate)
    bp = x_prompt.shape[0]
    zeros = lambda *s: jnp.zeros((DEPTH, bp) + s, x_prompt.dtype)
    prompt_init = (zeros(A_HEADS, A_HEAD, A_HEAD), zeros(A_SHIFT_W), zeros(B_HEADS, B_HEAD, B_HEAD),
                   zeros(B_CONV - 1, B_QKV), zeros(FFN_CONV - 1, D_FF))
    y_prompt, prompt_states = _trunk(x_prompt, p_prompt[0], prompt_init, weights)
    sample_init = (state_a_wkv, state_a_shift, state_b_ssm, state_b_conv, state_ffn_conv)
    y_sample, sample_states = _trunk(x_sample, p_sample[0], sample_init, weights)
    return (y_prompt, y_sample) + prompt_states + sample_states
```

```python
import functools

import jax
import jax.numpy as jnp
from jax import lax
from jax.experimental import pallas as pl
from jax.experimental.pallas import tpu as pltpu

F32 = jnp.float32
BF16 = jnp.bfloat16

LANES = 128
SUBLANES = 8
VMEM_LIMIT_BYTES = 56 * 1024 * 1024

D_MODEL = 1024
A_WIDTH = 512
A_HEAD = 64
A_HEADS = 8
A_SHIFT_W = 1792
B_WIDTH = 512
B_HEAD = 128
B_HEADS = 4
B_CONV = 4
B_QKV = 1536
B_PROJ_PAD = 2176
D_FF = 2816
FFN_CONV = 3
PLE_DIM = 256
DEPTH = 1
DN_ALPHA = (2.0 * DEPTH) ** 0.25
LN_EPS = 1e-5
GN_EPS = 64e-5
RMS_EPS = 1e-6
L2_EPS = 1e-12

SLOT_ROWS = 64
POST_ROWS = 256
PROJ_ROWS = 512
RWKV_SLOTS = (4, 2)
GDN_SLOTS = (8, 2)


def _dot(a, b):
    return jnp.dot(a, b, preferred_element_type=F32)


def _mm(a, b):
    return _dot(a.astype(BF16), b.astype(BF16))


def _split(x, n):
    parts = []
    r = x
    for i in range(n):
        h = r.astype(BF16)
        parts.append(h)
        if i + 1 < n:
            r = r - h.astype(F32)
    return parts


def _mm_exact_lhs(m, x, n):
    out = None
    for part in _split(x, n):
        d = _dot(m, part)
        out = d if out is None else out + d
    return out


def _mm_exact_rhs(x, m, n):
    out = None
    for part in _split(x, n):
        d = _dot(part, m)
        out = d if out is None else out + d
    return out


def _sigmoid(x):
    return 1.0 / (1.0 + jnp.exp(-x))


def _silu(x):
    return x * _sigmoid(x)


def _softplus(x):
    return jnp.maximum(x, 0.0) + jnp.log1p(jnp.exp(-jnp.abs(x)))


def _iota(shape, dim):
    return lax.broadcasted_iota(jnp.int32, shape, dim)


def _shift_rows(x, j, prev, c):
    rows = x.shape[0]
    rolled = pltpu.roll(x, j, 0)
    if c == SUBLANES:
        t = jnp.bitwise_and(_iota((rows, 1), 0), SUBLANES - 1)
        return jnp.where(t >= j, rolled, pltpu.roll(prev, rows - SUBLANES + j, 0))
    t = _iota((SUBLANES, 1), 0)
    parts = []
    for i in range(rows // c):
        head = jnp.where(t >= j, rolled[i * c:i * c + SUBLANES],
                         pltpu.roll(prev[i * SUBLANES:(i + 1) * SUBLANES], j, 0))
        parts += [head, rolled[i * c + SUBLANES:(i + 1) * c]]
    return jnp.concatenate(parts, axis=0)


def _load_prev(prev_scr, buf_ref, k_prev):
    prev_scr[...] = jnp.zeros(prev_scr.shape, F32)
    for i in range(buf_ref.shape[0]):
        prev_scr[SUBLANES * (i + 1) - k_prev:SUBLANES * (i + 1), :] = buf_ref[i]


def _store_tails(x, c, k_prev, prev_scr, tail_ref):
    for i in range(x.shape[0] // c):
        tail_ref[i] = x[(i + 1) * c - k_prev:(i + 1) * c, :]
        if c > SUBLANES:
            prev_scr[i * SUBLANES:(i + 1) * SUBLANES, :] = x[(i + 1) * c - SUBLANES:(i + 1) * c, :]


def _seq_cumsum_matrix(rows, log_c):
    r = _iota((rows, rows), 0)
    q = _iota((rows, rows), 1)
    same = jnp.right_shift(r, log_c) == jnp.right_shift(q, log_c)
    return jnp.where(jnp.logical_and(same, r >= q), 1.0, 0.0).astype(BF16)


def _block_masks(n, log_c):
    row = _iota((n, n), 0)
    col = _iota((n, n), 1)
    blk = jnp.right_shift(row, log_c) == jnp.right_shift(col, log_c)
    strict = jnp.logical_and(blk, row > col)
    incl = jnp.logical_and(blk, row >= col)
    return strict, incl


def _unit_lower_inverses_minus_eye(n_list, log_c):
    tm_l = list(n_list)
    pw_l = list(n_list)
    for _ in range(log_c - 1):
        pw_l = [_mm(q, q) for q in pw_l]
        tm_l = [t + q + _mm(t, q) for t, q in zip(tm_l, pw_l)]
    return tm_l


def _proj_kernel(x_ref, wa_ref, wb_ref, pa_ref, pb_ref):
    xb = x_ref[...].astype(BF16)
    pa_ref[...] = _dot(xb, wa_ref[...])
    pb_ref[...] = _dot(xb, wb_ref[...])


def _const_spec(shape):
    zeros = (0,) * len(shape)
    return pl.BlockSpec(shape, lambda *_: zeros, pipeline_mode=pl.Buffered(1))


def _proj_call(x2, w_a, w_b):
    m = x2.shape[0]
    tm = PROJ_ROWS
    return pl.pallas_call(
        _proj_kernel,
        grid=(m // tm,),
        in_specs=[pl.BlockSpec((tm, D_MODEL), lambda i: (i, 0)),
                  _const_spec(w_a.shape), _const_spec(w_b.shape)],
        out_specs=[pl.BlockSpec((tm, A_SHIFT_W), lambda i: (i, 0)),
                   pl.BlockSpec((tm, B_PROJ_PAD), lambda i: (i, 0))],
        out_shape=[jax.ShapeDtypeStruct((m, A_SHIFT_W), F32),
                   jax.ShapeDtypeStruct((m, B_PROJ_PAD), F32)],
        compiler_params=pltpu.CompilerParams(
            dimension_semantics=("arbitrary",), vmem_limit_bytes=VMEM_LIMIT_BYTES),
        name="in_proj",
    )(x2, w_a, w_b)


def _rwkv_kernel(pa_ref, sh0_ref, s0_ref, mu_ref, vec_ref, ww2_ref, wa2_ref, wg2_ref,
                 o_ref, sfin_ref, shout_ref, s_scr, prev_scr, *, ns, nb, log_c, n_t):
    c = 1 << log_c
    rows = SLOT_ROWS
    n2 = 2 * rows
    pairs = A_HEADS // 2
    n_seq = ns * nb
    t_idx = pl.program_id(1)

    @pl.when(t_idx == 0)
    def _init():
        z = jnp.zeros((A_HEAD, A_HEAD), F32)
        for i in range(n_seq):
            for p in range(pairs):
                top = jnp.concatenate([s0_ref[i, 2 * p], z], axis=1)
                bot = jnp.concatenate([z, s0_ref[i, 2 * p + 1]], axis=1)
                s_scr[i, p] = jnp.concatenate([top, bot], axis=0).T
        _load_prev(prev_scr, sh0_ref, 1)

    u = pa_ref[...].reshape(ns * rows, A_SHIFT_W)
    xs = u + (_shift_rows(u, 1, prev_scr[...], c) - u) * mu_ref[...]
    _store_tails(u, c, 1, prev_scr, shout_ref)

    w0 = vec_ref[0:1, :]
    a0 = vec_ref[1:2, :]
    k_k = vec_ref[2:3, :]
    k_a = vec_ref[3:4, :]
    r_k = vec_ref[4:5, :]
    gn_g = vec_ref[5:6, :]
    gn_b = vec_ref[6:7, :]

    r = xs[:, 0:512]
    k = xs[:, 512:1024]
    v = xs[:, 1024:1536]
    wa = xs[:, 1536:1664]
    gd = xs[:, 1664:1792]

    wlog = -_softplus(-(w0 + _dot(jnp.tanh(wa).astype(BF16), ww2_ref[...]))) - 0.5
    lw = -jnp.exp(wlog)
    a = _sigmoid(a0 + _dot(wa.astype(BF16), wa2_ref[...]))
    g = _dot(_sigmoid(gd).astype(BF16), wg2_ref[...])

    ones_head = jnp.where(
        jnp.right_shift(_iota((LANES, LANES), 0), 6) == jnp.right_shift(_iota((LANES, LANES), 1), 6),
        1.0, 0.0).astype(BF16)

    def head_sum(x):
        return jnp.concatenate(
            [_mm_exact_rhs(x[:, s * LANES:(s + 1) * LANES], ones_head, 2) for s in range(pairs)], axis=1)

    kkr = k * k_k
    kk = kkr * lax.rsqrt(head_sum(kkr * kkr) + L2_EPS)
    k2 = k * (1.0 + (a - 1.0) * k_a)
    bonus = head_sum(r * k2 * r_k) * v

    gcum = _mm_exact_lhs(_seq_cumsum_matrix(ns * rows, log_c), lw, 3)
    e_g = jnp.exp(gcum)
    e_gi = jnp.exp(-gcum)
    a_t = -kk * jnp.exp(gcum - lw)
    b_t = kk * a * e_gi
    k_t = k2 * e_gi
    r_t = r * e_g

    strict, incl = _block_masks(n2, log_c)
    lane0 = _iota((1, LANES), 1) < A_HEAD
    col_seq = jnp.bitwise_and(jnp.right_shift(_iota((1, n2), 1), log_c), nb - 1)

    def stack(xp):
        return jnp.concatenate([jnp.where(lane0, xp, 0.0), jnp.where(lane0, 0.0, xp)], axis=0)

    def seq_rows(x, b):
        if nb == 1:
            return x
        return jnp.concatenate([x[b * c:(b + 1) * c], x[rows + b * c:rows + (b + 1) * c]], axis=0)

    def seq_cols(xt, b):
        return xt if nb == 1 else jnp.where(col_seq == b, xt, 0.0)

    def restack(parts):
        if nb == 1:
            return parts[0]
        return jnp.concatenate([q[0:c] for q in parts] + [q[c:2 * c] for q in parts], axis=0)

    chains = [(s, p) for s in range(ns) for p in range(pairs)]
    a_l, r_l, v_l, bt_l, kt_l, n_l, ak_l, rbk_l, dec_l = ([] for _ in range(9))
    for s, p in chains:
        rs = slice(s * rows, (s + 1) * rows)
        sl = slice(p * LANES, (p + 1) * LANES)
        a_s, b_s, k_s, r_s, v_s = (stack(x[rs, sl]) for x in (a_t, b_t, k_t, r_t, v))
        bk_t = jnp.concatenate([b_s, k_s], axis=0).T
        x = _mm(jnp.concatenate([a_s, r_s], axis=0), bk_t)
        a_l.append(a_s)
        r_l.append(r_s.astype(BF16))
        v_l.append(v_s.astype(BF16))
        bt_l.append(bk_t[:, 0:n2])
        kt_l.append(bk_t[:, n2:2 * n2])
        n_l.append(jnp.where(strict, x[0:n2, 0:n2], 0.0))
        ak_l.append(jnp.where(strict, x[0:n2, n2:2 * n2], 0.0))
        rbk_l.append(jnp.concatenate([jnp.where(incl, x[n2:2 * n2, 0:n2], 0.0),
                                      jnp.where(incl, x[n2:2 * n2, n2:2 * n2], 0.0)], axis=1))
        dec_l.append(e_g[rs, sl].T)
    tm_l = _unit_lower_inverses_minus_eye(n_l, log_c)
    akv_l = [_mm(ak, v_s) for ak, v_s in zip(ak_l, v_l)]
    wu_l = []
    for ci in range(len(chains)):
        rhs = jnp.concatenate([a_l[ci], akv_l[ci]], axis=1)
        wu_l.append(rhs + _mm(tm_l[ci], rhs))
    kv_l = [[_mm(seq_cols(kt_l[ci], b), v_l[ci]) for b in range(nb)] for ci in range(len(chains))]
    sa_l, rs_l = [], []
    for ci, (s, p) in enumerate(chains):
        sa_parts, rs_parts = [], []
        for b in range(nb):
            m_old = s_scr[s * nb + b, p]
            sa_parts.append(_mm(seq_rows(wu_l[ci][:, 0:LANES], b), m_old)
                            + seq_rows(wu_l[ci][:, LANES:2 * LANES], b))
            rs_parts.append(_mm(seq_rows(r_l[ci], b), m_old))
        sa_l.append(restack(sa_parts))
        rs_l.append(restack(rs_parts))
    for ci, (s, p) in enumerate(chains):
        for b in range(nb):
            upd = _mm(seq_cols(bt_l[ci], b), sa_l[ci]) + kv_l[ci][b]
            s_scr[s * nb + b, p] = (s_scr[s * nb + b, p] + upd) * dec_l[ci][:, (b + 1) * c - 1:(b + 1) * c]
    o_rows = []
    for s in range(ns):
        o_parts = []
        for p in range(pairs):
            ci = s * pairs + p
            o_s = rs_l[ci] + _mm(rbk_l[ci], jnp.concatenate([sa_l[ci].astype(BF16), v_l[ci]], axis=0))
            o_parts.append(o_s[0:rows] + o_s[rows:n2])
        o_rows.append(jnp.concatenate(o_parts, axis=1))
    o = jnp.concatenate(o_rows, axis=0)

    mean = head_sum(o) * (1.0 / A_HEAD)
    d = o - mean
    var = head_sum(d * d) * (1.0 / A_HEAD)
    out = (d * lax.rsqrt(var + GN_EPS) * gn_g + gn_b + bonus) * g
    o_ref[...] = out.reshape(ns, rows, A_WIDTH)

    @pl.when(t_idx == n_t - 1)
    def _fin():
        for i in range(n_seq):
            for p in range(pairs):
                s_pair = s_scr[i, p].T
                sfin_ref[i, 2 * p] = s_pair[0:A_HEAD, 0:A_HEAD]
                sfin_ref[i, 2 * p + 1] = s_pair[A_HEAD:LANES, A_HEAD:LANES]


def _mixer_specs(ns, nb, width_in, width_out, state_block, tail_rows, tail_width):
    n_seq = ns * nb
    data = lambda g, t: (g, t, 0)
    grp3 = lambda g, t: (g, 0, 0)
    grp4 = lambda g, t: (g, 0, 0, 0)
    in_specs = [pl.BlockSpec((ns, SLOT_ROWS, width_in), data),
                pl.BlockSpec((n_seq, tail_rows, tail_width), grp3),
                pl.BlockSpec((n_seq,) + state_block, grp4)]
    out_specs = [pl.BlockSpec((ns, SLOT_ROWS, width_out), data),
                 pl.BlockSpec((n_seq,) + state_block, grp4),
                 pl.BlockSpec((n_seq, tail_rows, tail_width), grp3)]
    return in_specs, out_specs


def _rwkv_call(proj_a, shift0, wkv0, wts, ns, nb, log_c):
    n_t = proj_a.shape[1] // SLOT_ROWS
    prev_rows = (SUBLANES if nb == 1 else SLOT_ROWS) * ns
    in_specs, out_specs = _mixer_specs(ns, nb, A_SHIFT_W, A_WIDTH, (A_HEADS, A_HEAD, A_HEAD), 1, A_SHIFT_W)
    kern = functools.partial(_rwkv_kernel, ns=ns, nb=nb, log_c=log_c, n_t=n_t)
    return pl.pallas_call(
        kern,
        grid=(proj_a.shape[0] // ns, n_t),
        in_specs=in_specs + [_const_spec(w.shape) for w in wts],
        out_specs=out_specs,
        out_shape=[jax.ShapeDtypeStruct(proj_a.shape[:2] + (A_WIDTH,), F32),
                   jax.ShapeDtypeStruct(wkv0.shape, F32),
                   jax.ShapeDtypeStruct(shift0.shape, F32)],
        scratch_shapes=[pltpu.VMEM((ns * nb, A_HEADS // 2, LANES, LANES), F32),
                        pltpu.VMEM((prev_rows, A_SHIFT_W), F32)],
        compiler_params=pltpu.CompilerParams(
            dimension_semantics=("arbitrary", "arbitrary"), vmem_limit_bytes=VMEM_LIMIT_BYTES),
        name="rwkv7_mix",
    )(proj_a, shift0, wkv0, *wts)


def _gdn_kernel(pb_ref, cb_ref, s0_ref, cw_ref, hv_ref,
                o_ref, sfin_ref, cout_ref, s_scr, prev_scr, *, ns, nb, log_c, n_t):
    c = 1 << log_c
    rows = SLOT_ROWS
    n2 = 2 * rows
    units = 2 * nb
    npairs = B_HEADS // 2
    t_idx = pl.program_id(1)

    @pl.when(t_idx == 0)
    def _init():
        s_scr[...] = s0_ref[...]
        _load_prev(prev_scr, cb_ref, B_CONV - 1)

    pb = pb_ref[...].reshape(ns * rows, B_PROJ_PAD)
    x = pb[:, 0:B_QKV]
    z = pb[:, B_QKV:B_QKV + B_WIDTH]
    ba = pb[:, B_QKV + B_WIDTH:B_PROJ_PAD]
    prev = prev_scr[...]
    conv = (cw_ref[0:1, :] * _shift_rows(x, 3, prev, c) + cw_ref[1:2, :] * _shift_rows(x, 2, prev, c)
            + cw_ref[2:3, :] * _shift_rows(x, 1, prev, c) + cw_ref[3:4, :] * x)
    _store_tails(x, c, B_CONV - 1, prev_scr, cout_ref)
    qkv = _silu(conv)

    a_log = hv_ref[0:1, :]
    dt_bias = hv_ref[1:2, :]
    norm_g = hv_ref[2:3, :]
    beta_t = _sigmoid(ba)
    g_t = -jnp.exp(a_log) * _softplus(ba + dt_bias)
    gcum_t = _mm_exact_lhs(_seq_cumsum_matrix(ns * rows, log_c), g_t, 3)

    def l2n(xh):
        return xh * lax.rsqrt(jnp.sum(xh * xh, axis=-1, keepdims=True) + L2_EPS)

    strict, incl = _block_masks(n2, log_c)
    row2 = _iota((n2, n2), 0)
    col2 = _iota((n2, n2), 1)
    last_sel_t = row2 == jnp.bitwise_or(col2, c - 1)
    col_unit = jnp.right_shift(_iota((1, n2), 1), log_c)

    chains = [(s, p) for s in range(ns) for p in range(npairs)]
    nl_l, rhs_l, attn_l, qdec_l, kdt_l, gl_l = ([] for _ in range(6))
    for s, p in chains:
        rs = slice(s * rows, (s + 1) * rows)
        hs = (2 * p, 2 * p + 1)
        q_s = jnp.concatenate([l2n(qkv[rs, h * LANES:(h + 1) * LANES]) * (B_HEAD ** -0.5) for h in hs], axis=0)
        k_s = jnp.concatenate([l2n(qkv[rs, B_WIDTH + h * LANES:B_WIDTH + (h + 1) * LANES]) for h in hs], axis=0)
        v_s = jnp.concatenate([qkv[rs, 2 * B_WIDTH + h * LANES:2 * B_WIDTH + (h + 1) * LANES] for h in hs], axis=0)
        beta_s = jnp.concatenate([beta_t[rs, h:h + 1] for h in hs], axis=0)
        g_s = jnp.concatenate([gcum_t[rs, B_HEADS + h:B_HEADS + h + 1] for h in hs], axis=0)
        gi = jnp.broadcast_to(g_s, (n2, n2))
        gj = gi.T
        dec = jnp.where(incl, jnp.exp(gi - gj), 0.0)
        kb = k_s * beta_s
        ks_t = k_s.T
        x = _mm(jnp.concatenate([kb, q_s], axis=0), ks_t)
        e_gs = jnp.exp(g_s)
        g_last = jnp.sum(jnp.where(last_sel_t, gi, 0.0), axis=0, keepdims=True)
        nl_l.append(-(x[0:n2] * jnp.where(strict, dec, 0.0)))
        attn_l.append((x[n2:2 * n2] * dec).astype(BF16))
        rhs_l.append(jnp.concatenate([v_s * beta_s, kb * e_gs], axis=1))
        qdec_l.append(q_s * e_gs)
        kdt_l.append(ks_t * jnp.exp(g_last - gj[0:1, :]))
        gl_l.append(jnp.exp(g_last))
    tm_l = _unit_lower_inverses_minus_eye(nl_l, log_c)
    uw_l = [rhs + _mm(tm, rhs) for tm, rhs in zip(tm_l, rhs_l)]

    def state_index(s, p, unit):
        return s * nb + unit % nb, 2 * p + unit // nb

    vn_l, qs_l = [], []
    for ci, (s, p) in enumerate(chains):
        vn_parts, qs_parts = [], []
        for unit in range(units):
            s_old = s_scr[state_index(s, p, unit)]
            vn_parts.append(uw_l[ci][unit * c:(unit + 1) * c, 0:LANES]
                            - _mm(uw_l[ci][unit * c:(unit + 1) * c, LANES:2 * LANES], s_old))
            qs_parts.append(_mm(qdec_l[ci][unit * c:(unit + 1) * c], s_old))
        vn_l.append(jnp.concatenate(vn_parts, axis=0).astype(BF16))
        qs_l.append(jnp.concatenate(qs_parts, axis=0))
    for ci, (s, p) in enumerate(chains):
        for unit in range(units):
            kd_u = jnp.where(col_unit == unit, kdt_l[ci], 0.0)
            idx = state_index(s, p, unit)
            s_scr[idx] = s_scr[idx] * gl_l[ci][:, unit * c:unit * c + 1] + _mm(kd_u, vn_l[ci])
    o_rows = []
    for s in range(ns):
        outs = []
        for p in range(npairs):
            ci = s * npairs + p
            o_s = qs_l[ci] + _mm(attn_l[ci], vn_l[ci])
            for hh in range(2):
                oh = o_s[hh * rows:(hh + 1) * rows]
                on = oh * lax.rsqrt(jnp.mean(oh * oh, axis=-1, keepdims=True) + RMS_EPS) * norm_g
                h = 2 * p + hh
                outs.append(on * _silu(z[s * rows:(s + 1) * rows, h * LANES:(h + 1) * LANES]))
        o_rows.append(jnp.concatenate(outs, axis=1))
    o_ref[...] = jnp.concatenate(o_rows, axis=0).reshape(ns, rows, B_WIDTH)

    @pl.when(t_idx == n_t - 1)
    def _fin():
        sfin_ref[...] = s_scr[...]


def _gdn_call(proj_b, conv0, ssm0, wts, ns, nb, log_c):
    n_t = proj_b.shape[1] // SLOT_ROWS
    prev_rows = (SUBLANES if nb == 1 else SLOT_ROWS) * ns
    in_specs, out_specs = _mixer_specs(ns, nb, B_PROJ_PAD, B_WIDTH, (B_HEADS, B_HEAD, B_HEAD),
                                       B_CONV - 1, B_QKV)
    kern = functools.partial(_gdn_kernel, ns=ns, nb=nb, log_c=log_c, n_t=n_t)
    return pl.pallas_call(
        kern,
        grid=(proj_b.shape[0] // ns, n_t),
        in_specs=in_specs + [_const_spec(w.shape) for w in wts],
        out_specs=out_specs,
        out_shape=[jax.ShapeDtypeStruct(proj_b.shape[:2] + (B_WIDTH,), F32),
                   jax.ShapeDtypeStruct(ssm0.shape, F32),
                   jax.ShapeDtypeStruct(conv0.shape, F32)],
        scratch_shapes=[pltpu.VMEM((ns * nb, B_HEADS, B_HEAD, B_HEAD), F32),
                        pltpu.VMEM((prev_rows, B_QKV), F32)],
        compiler_params=pltpu.CompilerParams(
            dimension_semantics=("arbitrary", "arbitrary"), vmem_limit_bytes=VMEM_LIMIT_BYTES),
        name="gdn_mix",
    )(proj_b, conv0, ssm0, *wts)


def _layer_norm(x, g, b):
    mu = jnp.mean(x, axis=-1, keepdims=True)
    d = x - mu
    var = jnp.mean(d * d, axis=-1, keepdims=True)
    return d * lax.rsqrt(var + LN_EPS) * g + b


def _post_kernel(oa_ref, ob_ref, x_ref, p_ref, fb_ref, woa_ref, wob_ref, lnv_ref, wg_ref, wu_ref,
                 fv_ref, wd_ref, wple_ref, wpg_ref, y_ref, fout_ref, prev_scr, *, c):
    t_idx = pl.program_id(1)

    @pl.when(t_idx == 0)
    def _init():
        _load_prev(prev_scr, fb_ref, FFN_CONV - 1)

    ln1_g, ln1_b = lnv_ref[0:1, :], lnv_ref[1:2, :]
    ln2_g, ln2_b = lnv_ref[2:3, :], lnv_ref[3:4, :]
    ple_g = lnv_ref[4:5, :]

    mix = _dot(oa_ref[...].astype(BF16), woa_ref[...]) + _dot(ob_ref[...].astype(BF16), wob_ref[...])
    h = _layer_norm(DN_ALPHA * x_ref[...] + mix, ln1_g, ln1_b)
    hb = h.astype(BF16)
    gate = _dot(hb, wg_ref[...])
    up = _dot(hb, wu_ref[...])
    prev = prev_scr[...]
    gate_c = (fv_ref[0:1, :] * _shift_rows(gate, 2, prev, c) + fv_ref[1:2, :] * _shift_rows(gate, 1, prev, c)
              + fv_ref[2:3, :] * gate)
    _store_tails(gate, c, FFN_CONV - 1, prev_scr, fout_ref)
    act = _silu(gate_c + fv_ref[3:4, :]) * up
    ffn = _dot(act.astype(BF16), wd_ref[...])
    h2 = _layer_norm(DN_ALPHA * h + ffn, ln2_g, ln2_b)
    pe = _dot(p_ref[...].astype(BF16), wple_ref[...])
    e = pe * lax.rsqrt(jnp.mean(pe * pe, axis=-1, keepdims=True) + RMS_EPS) * ple_g
    y_ref[...] = h2 + _sigmoid(_dot(h2.astype(BF16), wpg_ref[...])) * e


def _post_call(o_a, o_b, x2, p2, fconv0, wts, seq):
    rows = POST_ROWS
    c = rows if seq % rows == 0 else seq
    nb = rows // c
    n_t = seq // c
    groups = fconv0.shape[0] // nb
    prev_rows = SUBLANES if nb == 1 else rows
    row_map = lambda g, t: (g * n_t + t, 0)
    grp3 = lambda g, t: (g, 0, 0)
    kern = functools.partial(_post_kernel, c=c)
    return pl.pallas_call(
        kern,
        grid=(groups, n_t),
        in_specs=[pl.BlockSpec((rows, A_WIDTH), row_map),
                  pl.BlockSpec((rows, B_WIDTH), row_map),
                  pl.BlockSpec((rows, D_MODEL), row_map),
                  pl.BlockSpec((rows, PLE_DIM), row_map),
                  pl.BlockSpec((nb, FFN_CONV - 1, D_FF), grp3)]
                 + [_const_spec(w.shape) for w in wts],
        out_specs=[pl.BlockSpec((rows, D_MODEL), row_map),
                   pl.BlockSpec((nb, FFN_CONV - 1, D_FF), grp3)],
        out_shape=[jax.ShapeDtypeStruct(x2.shape, F32),
                   jax.ShapeDtypeStruct(fconv0.shape, F32)],
        scratch_shapes=[pltpu.VMEM((prev_rows, D_FF), F32)],
        compiler_params=pltpu.CompilerParams(
            dimension_semantics=("arbitrary", "arbitrary"), vmem_limit_bytes=VMEM_LIMIT_BYTES),
        name="post_mix_ffn",
    )(o_a, o_b, x2, p2, fconv0, *wts)


def _pad_rows(w, n):
    return jnp.pad(w, ((0, n - w.shape[0]), (0, 0)))


def _pad_cols(w, n):
    return jnp.pad(w, ((0, 0), (0, n - w.shape[1])))


def _prepare_weights(w_in, a_mu, a_w0, a_w_w2, a_a0, a_w_a2, a_w_g2, a_k_k, a_k_a, a_r_k, a_gn_g,
                     a_gn_b, b_conv_w, b_a_log, b_dt_bias, b_norm_g, w_o, ln1_g, ln1_b, w_up,
                     f_conv_w, f_conv_b, w_down, ln2_g, ln2_b, w_ple, ple_g, w_ple_gate):
    w_in = w_in[0]
    lora = 64
    proj = (w_in[:, :A_SHIFT_W].astype(BF16), _pad_cols(w_in[:, A_SHIFT_W:], B_PROJ_PAD).astype(BF16))
    rwkv = (a_mu,
            _pad_rows(jnp.concatenate([a_w0, a_a0, a_k_k, a_k_a, a_r_k, a_gn_g, a_gn_b], axis=0), SUBLANES),
            _pad_rows(a_w_w2[0], LANES).astype(BF16),
            jnp.pad(a_w_a2[0], ((lora, LANES - 2 * lora), (0, 0))).astype(BF16),
            a_w_g2[0].astype(BF16))
    head_row = lambda vec: jnp.pad(vec, ((0, 0), (B_HEADS, LANES - 2 * B_HEADS)))
    gdn = (_pad_rows(b_conv_w[0], SUBLANES),
           _pad_rows(jnp.concatenate([head_row(b_a_log), head_row(b_dt_bias), b_norm_g], axis=0), SUBLANES))
    post = (w_o[0, :A_WIDTH].astype(BF16), w_o[0, A_WIDTH:].astype(BF16),
            _pad_rows(jnp.concatenate([ln1_g, ln1_b, ln2_g, ln2_b, ple_g], axis=0), SUBLANES),
            w_up[0, :, :D_FF].astype(BF16), w_up[0, :, D_FF:].astype(BF16),
            _pad_rows(jnp.concatenate([f_conv_w[0], f_conv_b], axis=0), SUBLANES),
            w_down[0].astype(BF16), w_ple[0].astype(BF16), w_ple_gate[0].astype(BF16))
    return proj, rwkv, gdn, post


def _trunk(x, p, states, weights):
    proj_w, rwkv_w, gdn_w, post_w = weights
    a_wkv, a_shift, b_ssm, b_conv, f_conv = (s[0] for s in states)
    bsz, seq, _ = x.shape
    m = bsz * seq
    x2 = x.reshape(m, D_MODEL)
    p2 = p.reshape(m, PLE_DIM)
    if seq % SLOT_ROWS == 0:
        nb, log_c, slots, kind = 1, 6, bsz, 0
    else:
        assert seq == SUBLANES
        nb, log_c, slots, kind = SLOT_ROWS // seq, 3, m // SLOT_ROWS, 1
    proj_a, proj_b = _proj_call(x2, *proj_w)
    o_a, wkv_new, shift_new = _rwkv_call(proj_a.reshape(slots, -1, A_SHIFT_W), a_shift[:, None, :], a_wkv,
                                         rwkv_w, RWKV_SLOTS[kind], nb, log_c)
    o_b, ssm_new, conv_new = _gdn_call(proj_b.reshape(slots, -1, B_PROJ_PAD), b_conv, b_ssm,
                                       gdn_w, GDN_SLOTS[kind], nb, log_c)
    y2, fconv_new = _post_call(o_a.reshape(m, A_WIDTH), o_b.reshape(m, B_WIDTH), x2, p2, f_conv, post_w, seq)
    new_states = (wkv_new, shift_new[:, 0, :], ssm_new, conv_new, fconv_new)
    return y2.reshape(bsz, seq, D_MODEL), tuple(s[None] for s in new_states)


def kernel(x_prompt, x_sample, p_prompt, p_sample, state_a_wkv, state_a_shift, state_b_ssm, state_b_conv, state_ffn_conv, w_in, a_mu, a_w0, a_w_w2, a_a0, a_w_a2, a_w_g2, a_k_k, a_k_a, a_r_k, a_gn_g, a_gn_b, b_conv_w, b_a_log, b_dt_bias, b_norm_g, w_o, ln1_g, ln1_b, w_up, f_conv_w, f_conv_b, w_down, ln2_g, ln2_b, w_ple, ple_g, w_ple_gate):
    assert w_in.shape[0] == DEPTH
    weights = _prepare_weights(w_in, a_mu, a_w0, a_w_w2, a_a0, a_w_a2, a_w_g2, a_k_k, a_k_a, a_r_k,
                               a_gn_g, a_gn_b, b_conv_w, b_a_log, b_dt_bias, b_norm_g, w_o, ln1_g,
                               ln1_b, w_up, f_conv_w, f_conv_b, w_down, ln2_g, ln2_b, w_ple, ple_g,
                               w_ple_gate)
    bp = x_prompt.shape[0]
    zeros = lambda *s: jnp.zeros((DEPTH, bp) + s, x_prompt.dtype)
    prompt_init = (zeros(A_HEADS, A_HEAD, A_HEAD), zeros(A_SHIFT_W), zeros(B_HEADS, B_HEAD, B_HEAD),
                   zeros(B_CONV - 1, B_QKV), zeros(FFN_CONV - 1, D_FF))
    y_prompt, prompt_states = _trunk(x_prompt, p_prompt[0], prompt_init, weights)
    sample_init = (state_a_wkv, state_a_shift, state_b_ssm, state_b_conv, state_ffn_conv)
    y_sample, sample_states = _trunk(x_sample, p_sample[0], sample_init, weights)
    return (y_prompt, y_sample) + prompt_states + sample_states
```

```python
import functools

import jax
import jax.numpy as jnp
from jax import lax
from jax.experimental import pallas as pl
from jax.experimental.pallas import tpu as pltpu

F32 = jnp.float32
BF16 = jnp.bfloat16

LANES = 128
SUBLANES = 8
MXU_TILE = 256
VMEM_LIMIT_BYTES = 56 * 1024 * 1024

D_MODEL = 1024
A_WIDTH = 512
A_HEAD = 64
A_HEADS = 8
A_SHIFT_W = 1792
B_WIDTH = 512
B_HEAD = 128
B_HEADS = 4
B_CONV = 4
B_QKV = 1536
B_PROJ_PAD = 2176
D_FF = 2816
FFN_CONV = 3
PLE_DIM = 256
DEPTH = 1
DN_ALPHA = (2.0 * DEPTH) ** 0.25
LN_EPS = 1e-5
GN_EPS = 64e-5
RMS_EPS = 1e-6
L2_EPS = 1e-12

SLOT_ROWS = 64
POST_ROWS = (512, 256)
PROJ_ROWS = 512
FF_CUTS = (0, 1536, 2816)
RWKV_SLOTS = (4, 2)
GDN_SLOTS = (8, 2)


def _dot(a, b):
    return jnp.dot(a, b, preferred_element_type=F32)


def _mm(a, b):
    return _dot(a.astype(BF16), b.astype(BF16))


def _split(x, n):
    parts = []
    r = x
    for i in range(n):
        h = r.astype(BF16)
        parts.append(h)
        if i + 1 < n:
            r = r - h.astype(F32)
    return parts


def _mm_exact_lhs(m, x, n):
    out = None
    for part in _split(x, n):
        d = _dot(m, part)
        out = d if out is None else out + d
    return out


def _mm_exact_rhs(x, m, n):
    out = None
    for part in _split(x, n):
        d = _dot(part, m)
        out = d if out is None else out + d
    return out


def _segment_sums(x, log_seg, n_split):
    tile = min(MXU_TILE, x.shape[1])
    same = (jnp.right_shift(_iota((tile, tile), 0), log_seg) == jnp.right_shift(_iota((tile, tile), 1), log_seg))
    ones = jnp.where(same, 1.0, 0.0).astype(BF16)
    parts = [_mm_exact_rhs(x[:, s:s + tile], ones, n_split) for s in range(0, x.shape[1], tile)]
    return parts[0] if len(parts) == 1 else jnp.concatenate(parts, axis=1)


def _sigmoid(x):
    return 1.0 / (1.0 + jnp.exp(-x))


def _silu(x):
    return x * _sigmoid(x)


def _softplus(x):
    return jnp.maximum(x, 0.0) + jnp.log1p(jnp.exp(-jnp.abs(x)))


def _iota(shape, dim):
    return lax.broadcasted_iota(jnp.int32, shape, dim)


def _shift_rows(x, j, prev, c):
    rows = x.shape[0]
    rolled = pltpu.roll(x, j, 0)
    if c == SUBLANES:
        t = jnp.bitwise_and(_iota((rows, 1), 0), SUBLANES - 1)
        return jnp.where(t >= j, rolled, pltpu.roll(prev, rows - SUBLANES + j, 0))
    t = _iota((SUBLANES, 1), 0)
    parts = []
    for i in range(rows // c):
        head = jnp.where(t >= j, rolled[i * c:i * c + SUBLANES],
                         pltpu.roll(prev[i * SUBLANES:(i + 1) * SUBLANES], j, 0))
        parts += [head, rolled[i * c + SUBLANES:(i + 1) * c]]
    return jnp.concatenate(parts, axis=0)


def _load_prev(prev_scr, buf_ref, k_prev):
    prev_scr[...] = jnp.zeros(prev_scr.shape, F32)
    for i in range(buf_ref.shape[0]):
        prev_scr[SUBLANES * (i + 1) - k_prev:SUBLANES * (i + 1), :] = buf_ref[i]


def _store_tails(x, c, k_prev, prev_scr, tail_ref):
    for i in range(x.shape[0] // c):
        tail_ref[i] = x[(i + 1) * c - k_prev:(i + 1) * c, :]
        if c > SUBLANES:
            prev_scr[i * SUBLANES:(i + 1) * SUBLANES, :] = x[(i + 1) * c - SUBLANES:(i + 1) * c, :]


def _seq_cumsum_matrix(rows, log_c):
    r = _iota((rows, rows), 0)
    q = _iota((rows, rows), 1)
    same = jnp.right_shift(r, log_c) == jnp.right_shift(q, log_c)
    return jnp.where(jnp.logical_and(same, r >= q), 1.0, 0.0).astype(BF16)


def _block_masks(n, log_c):
    row = _iota((n, n), 0)
    col = _iota((n, n), 1)
    blk = jnp.right_shift(row, log_c) == jnp.right_shift(col, log_c)
    strict = jnp.logical_and(blk, row > col)
    incl = jnp.logical_and(blk, row >= col)
    return strict, incl


def _unit_lower_inverses_minus_eye(n_list, log_c):
    tm_l = list(n_list)
    pw_l = list(n_list)
    for _ in range(log_c - 1):
        pw_l = [_mm(q, q) for q in pw_l]
        tm_l = [t + q + _mm(t, q) for t, q in zip(tm_l, pw_l)]
    return tm_l


def _proj_kernel(x_ref, w_ref, pa_ref, pb_ref):
    xb = x_ref[...].astype(BF16)
    in_w = w_ref.shape[1]
    b_w = in_w - A_SHIFT_W
    pa_ref[...] = _dot(xb, w_ref[:, 0:A_SHIFT_W])
    pb_ref[:, 0:b_w] = _dot(xb, w_ref[:, A_SHIFT_W:in_w])
    pb_ref[:, b_w:B_PROJ_PAD] = jnp.zeros((pb_ref.shape[0], B_PROJ_PAD - b_w), F32)


def _const_spec(shape):
    zeros = (0,) * len(shape)
    return pl.BlockSpec(shape, lambda *_: zeros, pipeline_mode=pl.Buffered(1))


def _proj_call(x2, w_in):
    m = x2.shape[0]
    tm = PROJ_ROWS
    return pl.pallas_call(
        _proj_kernel,
        grid=(m // tm,),
        in_specs=[pl.BlockSpec((tm, D_MODEL), lambda i: (i, 0)), _const_spec(w_in.shape)],
        out_specs=[pl.BlockSpec((tm, A_SHIFT_W), lambda i: (i, 0)),
                   pl.BlockSpec((tm, B_PROJ_PAD), lambda i: (i, 0))],
        out_shape=[jax.ShapeDtypeStruct((m, A_SHIFT_W), F32),
                   jax.ShapeDtypeStruct((m, B_PROJ_PAD), F32)],
        compiler_params=pltpu.CompilerParams(
            dimension_semantics=("arbitrary",), vmem_limit_bytes=VMEM_LIMIT_BYTES),
        name="in_proj",
    )(x2, w_in)


def _rwkv_kernel(pa_ref, sh0_ref, s0_ref, mu_ref, vec_ref, ww2_ref, wa2_ref, wg2_ref,
                 o_ref, sfin_ref, shout_ref, s_scr, prev_scr, *, ns, nb, log_c, n_t):
    c = 1 << log_c
    rows = SLOT_ROWS
    n2 = 2 * rows
    pairs = A_HEADS // 2
    n_seq = ns * nb
    t_idx = pl.program_id(1)

    @pl.when(t_idx == 0)
    def _init():
        z = jnp.zeros((A_HEAD, A_HEAD), F32)
        for i in range(n_seq):
            for p in range(pairs):
                top = jnp.concatenate([s0_ref[i, 2 * p], z], axis=1)
                bot = jnp.concatenate([z, s0_ref[i, 2 * p + 1]], axis=1)
                s_scr[i, p] = jnp.concatenate([top, bot], axis=0).T
        _load_prev(prev_scr, sh0_ref, 1)

    u = pa_ref[...].reshape(ns * rows, A_SHIFT_W)
    xs = u + (_shift_rows(u, 1, prev_scr[...], c) - u) * mu_ref[...]
    _store_tails(u, c, 1, prev_scr, shout_ref)

    w0 = vec_ref[0:1, :]
    a0 = vec_ref[1:2, :]
    k_k = vec_ref[2:3, :]
    k_a = vec_ref[3:4, :]
    r_k = vec_ref[4:5, :]
    gn_g = vec_ref[5:6, :]
    gn_b = vec_ref[6:7, :]

    r = xs[:, 0:512]
    k = xs[:, 512:1024]
    v = xs[:, 1024:1536]
    wa = xs[:, 1536:1664]
    gd = xs[:, 1664:1792]

    wlog = -_softplus(-(w0 + _dot(jnp.tanh(wa).astype(BF16), ww2_ref[...]))) - 0.5
    lw = -jnp.exp(wlog)
    a = _sigmoid(a0 + _dot(wa.astype(BF16), wa2_ref[...]))
    g = _dot(_sigmoid(gd).astype(BF16), wg2_ref[...])

    def head_sum(x, n_split):
        return _segment_sums(x, 6, n_split)

    kkr = k * k_k
    kk = kkr * lax.rsqrt(head_sum(kkr * kkr, 1) + L2_EPS)
    k2 = k * (1.0 + (a - 1.0) * k_a)
    bonus = head_sum(r * k2 * r_k, 1) * v

    gcum = _mm_exact_lhs(_seq_cumsum_matrix(ns * rows, log_c), lw, 2)
    e_g = jnp.exp(gcum)
    e_gi = jnp.exp(-gcum)
    a_t = -kk * jnp.exp(gcum - lw)
    b_t = kk * a * e_gi
    k_t = k2 * e_gi
    r_t = r * e_g

    strict, incl = _block_masks(n2, log_c)
    lane0 = _iota((1, LANES), 1) < A_HEAD
    col_seq = jnp.bitwise_and(jnp.right_shift(_iota((1, n2), 1), log_c), nb - 1)

    def stack(xp):
        return jnp.concatenate([jnp.where(lane0, xp, 0.0), jnp.where(lane0, 0.0, xp)], axis=0)

    def seq_rows(x, b):
        if nb == 1:
            return x
        return jnp.concatenate([x[b * c:(b + 1) * c], x[rows + b * c:rows + (b + 1) * c]], axis=0)

    def seq_cols(xt, b):
        return xt if nb == 1 else jnp.where(col_seq == b, xt, 0.0)

    def restack(parts):
        if nb == 1:
            return parts[0]
        return jnp.concatenate([q[0:c] for q in parts] + [q[c:2 * c] for q in parts], axis=0)

    chains = [(s, p) for s in range(ns) for p in range(pairs)]
    a_l, r_l, v_l, bt_l, kt_l, n_l, ak_l, rbk_l, dec_l = ([] for _ in range(9))
    for s, p in chains:
        rs = slice(s * rows, (s + 1) * rows)
        sl = slice(p * LANES, (p + 1) * LANES)
        a_s, b_s, k_s, r_s, v_s = (stack(x[rs, sl]) for x in (a_t, b_t, k_t, r_t, v))
        bk_t = jnp.concatenate([b_s, k_s], axis=0).T
        x = _mm(jnp.concatenate([a_s, r_s], axis=0), bk_t)
        a_l.append(a_s)
        r_l.append(r_s.astype(BF16))
        v_l.append(v_s.astype(BF16))
        bt_l.append(bk_t[:, 0:n2])
        kt_l.append(bk_t[:, n2:2 * n2])
        n_l.append(jnp.where(strict, x[0:n2, 0:n2], 0.0))
        ak_l.append(jnp.where(strict, x[0:n2, n2:2 * n2], 0.0))
        rbk_l.append(jnp.concatenate([jnp.where(incl, x[n2:2 * n2, 0:n2], 0.0),
                                      jnp.where(incl, x[n2:2 * n2, n2:2 * n2], 0.0)], axis=1))
        dec_l.append(e_g[rs, sl].T)
    tm_l = _unit_lower_inverses_minus_eye(n_l, log_c)
    akv_l = [_mm(ak, v_s) for ak, v_s in zip(ak_l, v_l)]
    wu_l = []
    for ci in range(len(chains)):
        rhs = jnp.concatenate([a_l[ci], akv_l[ci]], axis=1)
        wu_l.append(rhs + _mm(tm_l[ci], rhs))
    kv_l = [[_mm(seq_cols(kt_l[ci], b), v_l[ci]) for b in range(nb)] for ci in range(len(chains))]
    sa_l, rs_l = [], []
    for ci, (s, p) in enumerate(chains):
        sa_parts, rs_parts = [], []
        for b in range(nb):
            m_old = s_scr[s * nb + b, p]
            sa_parts.append(_mm(seq_rows(wu_l[ci][:, 0:LANES], b), m_old)
                            + seq_rows(wu_l[ci][:, LANES:2 * LANES], b))
            rs_parts.append(_mm(seq_rows(r_l[ci], b), m_old))
        sa_l.append(restack(sa_parts))
        rs_l.append(restack(rs_parts))
    for ci, (s, p) in enumerate(chains):
        for b in range(nb):
            upd = _mm(seq_cols(bt_l[ci], b), sa_l[ci]) + kv_l[ci][b]
            s_scr[s * nb + b, p] = (s_scr[s * nb + b, p] + upd) * dec_l[ci][:, (b + 1) * c - 1:(b + 1) * c]
    o_rows = []
    for s in range(ns):
        o_parts = []
        for p in range(pairs):
            ci = s * pairs + p
            o_s = rs_l[ci] + _mm(rbk_l[ci], jnp.concatenate([sa_l[ci].astype(BF16), v_l[ci]], axis=0))
            o_parts.append(o_s[0:rows] + o_s[rows:n2])
        o_rows.append(jnp.concatenate(o_parts, axis=1))
    o = jnp.concatenate(o_rows, axis=0)

    mean = head_sum(o, 2) * (1.0 / A_HEAD)
    d = o - mean
    var = head_sum(d * d, 1) * (1.0 / A_HEAD)
    out = (d * lax.rsqrt(var + GN_EPS) * gn_g + gn_b + bonus) * g
    o_ref[...] = out.reshape(ns, rows, A_WIDTH)

    @pl.when(t_idx == n_t - 1)
    def _fin():
        for i in range(n_seq):
            for p in range(pairs):
                s_pair = s_scr[i, p].T
                sfin_ref[i, 2 * p] = s_pair[0:A_HEAD, 0:A_HEAD]
                sfin_ref[i, 2 * p + 1] = s_pair[A_HEAD:LANES, A_HEAD:LANES]


def _mixer_specs(ns, nb, width_in, width_out, state_block, tail_rows, tail_width):
    n_seq = ns * nb
    data = lambda g, t: (g, t, 0)
    grp3 = lambda g, t: (g, 0, 0)
    grp4 = lambda g, t: (g, 0, 0, 0)
    in_specs = [pl.BlockSpec((ns, SLOT_ROWS, width_in), data),
                pl.BlockSpec((n_seq, tail_rows, tail_width), grp3),
                pl.BlockSpec((n_seq,) + state_block, grp4)]
    out_specs = [pl.BlockSpec((ns, SLOT_ROWS, width_out), data),
                 pl.BlockSpec((n_seq,) + state_block, grp4),
                 pl.BlockSpec((n_seq, tail_rows, tail_width), grp3)]
    return in_specs, out_specs


def _rwkv_call(proj_a, shift0, wkv0, wts, ns, nb, log_c):
    n_t = proj_a.shape[1] // SLOT_ROWS
    prev_rows = (SUBLANES if nb == 1 else SLOT_ROWS) * ns
    in_specs, out_specs = _mixer_specs(ns, nb, A_SHIFT_W, A_WIDTH, (A_HEADS, A_HEAD, A_HEAD), 1, A_SHIFT_W)
    kern = functools.partial(_rwkv_kernel, ns=ns, nb=nb, log_c=log_c, n_t=n_t)
    return pl.pallas_call(
        kern,
        grid=(proj_a.shape[0] // ns, n_t),
        in_specs=in_specs + [_const_spec(w.shape) for w in wts],
        out_specs=out_specs,
        out_shape=[jax.ShapeDtypeStruct(proj_a.shape[:2] + (A_WIDTH,), F32),
                   jax.ShapeDtypeStruct(wkv0.shape, F32),
                   jax.ShapeDtypeStruct(shift0.shape, F32)],
        scratch_shapes=[pltpu.VMEM((ns * nb, A_HEADS // 2, LANES, LANES), F32),
                        pltpu.VMEM((prev_rows, A_SHIFT_W), F32)],
        compiler_params=pltpu.CompilerParams(
            dimension_semantics=("arbitrary", "arbitrary"), vmem_limit_bytes=VMEM_LIMIT_BYTES),
        name="rwkv7_mix",
    )(proj_a, shift0, wkv0, *wts)


def _gdn_kernel(pb_ref, cb_ref, s0_ref, cw_ref, hv_ref,
                o_ref, sfin_ref, cout_ref, s_scr, prev_scr, *, ns, nb, log_c, n_t):
    c = 1 << log_c
    rows = SLOT_ROWS
    n2 = 2 * rows
    units = 2 * nb
    npairs = B_HEADS // 2
    t_idx = pl.program_id(1)

    @pl.when(t_idx == 0)
    def _init():
        s_scr[...] = s0_ref[...]
        _load_prev(prev_scr, cb_ref, B_CONV - 1)

    pb = pb_ref[...].reshape(ns * rows, B_PROJ_PAD)
    x = pb[:, 0:B_QKV]
    z = pb[:, B_QKV:B_QKV + B_WIDTH]
    ba = pb[:, B_QKV + B_WIDTH:B_PROJ_PAD]
    prev = prev_scr[...]
    conv = (cw_ref[0:1, :] * _shift_rows(x, 3, prev, c) + cw_ref[1:2, :] * _shift_rows(x, 2, prev, c)
            + cw_ref[2:3, :] * _shift_rows(x, 1, prev, c) + cw_ref[3:4, :] * x)
    _store_tails(x, c, B_CONV - 1, prev_scr, cout_ref)
    qkv = _silu(conv)

    a_log = hv_ref[0:1, :]
    dt_bias = hv_ref[1:2, :]
    norm_g = hv_ref[2:3, :]
    beta_t = _sigmoid(ba)
    g_t = -jnp.exp(a_log) * _softplus(ba + dt_bias)
    gcum_t = _mm_exact_lhs(_seq_cumsum_matrix(ns * rows, log_c), g_t, 3)

    qk = qkv[:, 0:2 * B_WIDTH]
    qk = qk * lax.rsqrt(_segment_sums(qk * qk, 7, 1) + L2_EPS)
    qn = qk[:, 0:B_WIDTH] * (B_HEAD ** -0.5)
    kn = qk[:, B_WIDTH:2 * B_WIDTH]

    strict, incl = _block_masks(n2, log_c)
    row2 = _iota((n2, n2), 0)
    col2 = _iota((n2, n2), 1)
    last_sel_t = row2 == jnp.bitwise_or(col2, c - 1)
    col_unit = jnp.right_shift(_iota((1, n2), 1), log_c)

    chains = [(s, p) for s in range(ns) for p in range(npairs)]
    nl_l, rhs_l, attn_l, qdec_l, kdt_l, gl_l = ([] for _ in range(6))
    for s, p in chains:
        rs = slice(s * rows, (s + 1) * rows)
        hs = (2 * p, 2 * p + 1)
        q_s = jnp.concatenate([qn[rs, h * LANES:(h + 1) * LANES] for h in hs], axis=0)
        k_s = jnp.concatenate([kn[rs, h * LANES:(h + 1) * LANES] for h in hs], axis=0)
        v_s = jnp.concatenate([qkv[rs, 2 * B_WIDTH + h * LANES:2 * B_WIDTH + (h + 1) * LANES] for h in hs], axis=0)
        beta_s = jnp.broadcast_to(jnp.concatenate([beta_t[rs, h:h + 1] for h in hs], axis=0), (n2, LANES))
        g_s = jnp.concatenate([gcum_t[rs, B_HEADS + h:B_HEADS + h + 1] for h in hs], axis=0)
        gi = jnp.broadcast_to(g_s, (n2, n2))
        gj = gi.T
        dec = jnp.where(incl, jnp.exp(gi - gj), 0.0)
        kb = k_s * beta_s
        ks_t = k_s.T
        x = _mm(jnp.concatenate([kb, q_s], axis=0), ks_t)
        e_gs = jnp.exp(gi)
        g_last = jnp.sum(jnp.where(last_sel_t, gi, 0.0), axis=0, keepdims=True)
        nl_l.append(-(x[0:n2] * jnp.where(strict, dec, 0.0)))
        attn_l.append((x[n2:2 * n2] * dec).astype(BF16))
        rhs_l.append(jnp.concatenate([v_s * beta_s, kb * e_gs], axis=1))
        qdec_l.append(q_s * e_gs)
        kdt_l.append(ks_t * jnp.exp(g_last - gj[0:1, :]))
        gl_l.append(jnp.exp(g_last))
    tm_l = _unit_lower_inverses_minus_eye(nl_l, log_c)
    uw_l = [rhs + _mm(tm, rhs) for tm, rhs in zip(tm_l, rhs_l)]

    def state_index(s, p, unit):
        return s * nb + unit % nb, 2 * p + unit // nb

    vn_l, qs_l = [], []
    for ci, (s, p) in enumerate(chains):
        vn_parts, qs_parts = [], []
        for unit in range(units):
            s_old = s_scr[state_index(s, p, unit)]
            vn_parts.append(uw_l[ci][unit * c:(unit + 1) * c, 0:LANES]
                            - _mm(uw_l[ci][unit * c:(unit + 1) * c, LANES:2 * LANES], s_old))
            qs_parts.append(_mm(qdec_l[ci][unit * c:(unit + 1) * c], s_old))
        vn_l.append(jnp.concatenate(vn_parts, axis=0).astype(BF16))
        qs_l.append(jnp.concatenate(qs_parts, axis=0))
    for ci, (s, p) in enumerate(chains):
        for unit in range(units):
            kd_u = jnp.where(col_unit == unit, kdt_l[ci], 0.0)
            idx = state_index(s, p, unit)
            s_scr[idx] = s_scr[idx] * gl_l[ci][:, unit * c:unit * c + 1] + _mm(kd_u, vn_l[ci])
    o_rows = []
    for s in range(ns):
        heads = []
        for p in range(npairs):
            ci = s * npairs + p
            o_s = qs_l[ci] + _mm(attn_l[ci], vn_l[ci])
            heads += [o_s[0:rows], o_s[rows:n2]]
        o_rows.append(jnp.concatenate(heads, axis=1))
    o = jnp.concatenate(o_rows, axis=0)
    mean_sq = _segment_sums(o * o, 7, 1) * (1.0 / B_HEAD)
    out = o * lax.rsqrt(mean_sq + RMS_EPS) * jnp.concatenate([norm_g] * B_HEADS, axis=1) * _silu(z)
    o_ref[...] = out.reshape(ns, rows, B_WIDTH)

    @pl.when(t_idx == n_t - 1)
    def _fin():
        sfin_ref[...] = s_scr[...]


def _gdn_call(proj_b, conv0, ssm0, wts, ns, nb, log_c):
    n_t = proj_b.shape[1] // SLOT_ROWS
    prev_rows = (SUBLANES if nb == 1 else SLOT_ROWS) * ns
    in_specs, out_specs = _mixer_specs(ns, nb, B_PROJ_PAD, B_WIDTH, (B_HEADS, B_HEAD, B_HEAD),
                                       B_CONV - 1, B_QKV)
    kern = functools.partial(_gdn_kernel, ns=ns, nb=nb, log_c=log_c, n_t=n_t)
    return pl.pallas_call(
        kern,
        grid=(proj_b.shape[0] // ns, n_t),
        in_specs=in_specs + [_const_spec(w.shape) for w in wts],
        out_specs=out_specs,
        out_shape=[jax.ShapeDtypeStruct(proj_b.shape[:2] + (B_WIDTH,), F32),
                   jax.ShapeDtypeStruct(ssm0.shape, F32),
                   jax.ShapeDtypeStruct(conv0.shape, F32)],
        scratch_shapes=[pltpu.VMEM((ns * nb, B_HEADS, B_HEAD, B_HEAD), F32),
                        pltpu.VMEM((prev_rows, B_QKV), F32)],
        compiler_params=pltpu.CompilerParams(
            dimension_semantics=("arbitrary", "arbitrary"), vmem_limit_bytes=VMEM_LIMIT_BYTES),
        name="gdn_mix",
    )(proj_b, conv0, ssm0, *wts)


def _layer_norm(x, g, b):
    mu = jnp.mean(x, axis=-1, keepdims=True)
    d = x - mu
    var = jnp.mean(d * d, axis=-1, keepdims=True)
    return d * lax.rsqrt(var + LN_EPS) * g + b


def _post_kernel(oa_ref, ob_ref, x_ref, p_ref, fb_ref, woa_ref, wob_ref, lnv_ref, wgu_ref,
                 fv_ref, wd_ref, wple_ref, wpg_ref, y_ref, fout_ref, prev_scr, *, c):
    t_idx = pl.program_id(1)

    @pl.when(t_idx == 0)
    def _init():
        _load_prev(prev_scr, fb_ref, FFN_CONV - 1)

    ln1_g, ln1_b = lnv_ref[0:1, :], lnv_ref[1:2, :]
    ln2_g, ln2_b = lnv_ref[2:3, :], lnv_ref[3:4, :]
    ple_g = lnv_ref[4:5, :]

    mix = _dot(oa_ref[...].astype(BF16), woa_ref[...]) + _dot(ob_ref[...].astype(BF16), wob_ref[...])
    h = _layer_norm(DN_ALPHA * x_ref[...] + mix, ln1_g, ln1_b)
    hb = h.astype(BF16)
    ffn = None
    for f0, f1 in zip(FF_CUTS[:-1], FF_CUTS[1:]):
        fs = slice(f0, f1)
        gate = _dot(hb, wgu_ref[:, fs])
        up = _dot(hb, wgu_ref[:, D_FF + f0:D_FF + f1])
        prev = prev_scr[:, fs]
        gate_c = (fv_ref[0:1, fs] * _shift_rows(gate, 2, prev, c) + fv_ref[1:2, fs] * _shift_rows(gate, 1, prev, c)
                  + fv_ref[2:3, fs] * gate)
        _store_tails(gate, c, FFN_CONV - 1, prev_scr.at[:, fs], fout_ref.at[:, :, fs])
        act = _silu(gate_c + fv_ref[3:4, fs]) * up
        part = _dot(act.astype(BF16), wd_ref[fs, :])
        ffn = part if ffn is None else ffn + part
    h2 = _layer_norm(DN_ALPHA * h + ffn, ln2_g, ln2_b)
    pe = _dot(p_ref[...].astype(BF16), wple_ref[...])
    e = pe * lax.rsqrt(jnp.mean(pe * pe, axis=-1, keepdims=True) + RMS_EPS) * ple_g
    y_ref[...] = h2 + _sigmoid(_dot(h2.astype(BF16), wpg_ref[...])) * e


def _post_call(o_a, o_b, x2, p2, fconv0, wts, seq):
    rows = POST_ROWS[0] if seq % POST_ROWS[0] == 0 else POST_ROWS[1]
    c = rows if seq % rows == 0 else seq
    nb = rows // c
    n_t = seq // c
    groups = fconv0.shape[0] // nb
    prev_rows = SUBLANES if nb == 1 else rows
    row_map = lambda g, t: (g * n_t + t, 0)
    grp3 = lambda g, t: (g, 0, 0)
    kern = functools.partial(_post_kernel, c=c)
    return pl.pallas_call(
        kern,
        grid=(groups, n_t),
        in_specs=[pl.BlockSpec((rows, A_WIDTH), row_map),
                  pl.BlockSpec((rows, B_WIDTH), row_map),
                  pl.BlockSpec((rows, D_MODEL), row_map),
                  pl.BlockSpec((rows, PLE_DIM), row_map),
                  pl.BlockSpec((nb, FFN_CONV - 1, D_FF), grp3)]
                 + [_const_spec(w.shape) for w in wts],
        out_specs=[pl.BlockSpec((rows, D_MODEL), row_map),
                   pl.BlockSpec((nb, FFN_CONV - 1, D_FF), grp3)],
        out_shape=[jax.ShapeDtypeStruct(x2.shape, F32),
                   jax.ShapeDtypeStruct(fconv0.shape, F32)],
        scratch_shapes=[pltpu.VMEM((prev_rows, D_FF), F32)],
        compiler_params=pltpu.CompilerParams(
            dimension_semantics=("arbitrary", "arbitrary"), vmem_limit_bytes=VMEM_LIMIT_BYTES),
        name="post_mix_ffn",
    )(o_a, o_b, x2, p2, fconv0, *wts)


def _pad_rows(w, n):
    return jnp.pad(w, ((0, n - w.shape[0]), (0, 0)))


def _pad_cols(w, n):
    return jnp.pad(w, ((0, 0), (0, n - w.shape[1])))


def _prepare_weights(w_in, a_mu, a_w0, a_w_w2, a_a0, a_w_a2, a_w_g2, a_k_k, a_k_a, a_r_k, a_gn_g,
                     a_gn_b, b_conv_w, b_a_log, b_dt_bias, b_norm_g, w_o, ln1_g, ln1_b, w_up,
                     f_conv_w, f_conv_b, w_down, ln2_g, ln2_b, w_ple, ple_g, w_ple_gate):
    lora = 64
    proj = (w_in[0].astype(BF16),)
    rwkv = (a_mu,
            _pad_rows(jnp.concatenate([a_w0, a_a0, a_k_k, a_k_a, a_r_k, a_gn_g, a_gn_b], axis=0), SUBLANES),
            _pad_rows(a_w_w2[0], LANES).astype(BF16),
            jnp.pad(a_w_a2[0], ((lora, LANES - 2 * lora), (0, 0))).astype(BF16),
            a_w_g2[0].astype(BF16))
    head_row = lambda vec: jnp.pad(vec, ((0, 0), (B_HEADS, LANES - 2 * B_HEADS)))
    gdn = (_pad_rows(b_conv_w[0], SUBLANES),
           _pad_rows(jnp.concatenate([head_row(b_a_log), head_row(b_dt_bias), b_norm_g], axis=0), SUBLANES))
    post = (w_o[0, :A_WIDTH].astype(BF16), w_o[0, A_WIDTH:].astype(BF16),
            _pad_rows(jnp.concatenate([ln1_g, ln1_b, ln2_g, ln2_b, ple_g], axis=0), SUBLANES),
            w_up[0].astype(BF16),
            _pad_rows(jnp.concatenate([f_conv_w[0], f_conv_b], axis=0), SUBLANES),
            w_down[0].astype(BF16), w_ple[0].astype(BF16), w_ple_gate[0].astype(BF16))
    return proj, rwkv, gdn, post


def _trunk(x, p, states, weights):
    proj_w, rwkv_w, gdn_w, post_w = weights
    a_wkv, a_shift, b_ssm, b_conv, f_conv = (s[0] for s in states)
    bsz, seq, _ = x.shape
    m = bsz * seq
    x2 = x.reshape(m, D_MODEL)
    p2 = p.reshape(m, PLE_DIM)
    if seq % SLOT_ROWS == 0:
        nb, log_c, slots, kind = 1, 6, bsz, 0
    else:
        assert seq == SUBLANES
        nb, log_c, slots, kind = SLOT_ROWS // seq, 3, m // SLOT_ROWS, 1
    proj_a, proj_b = _proj_call(x2, *proj_w)
    o_a, wkv_new, shift_new = _rwkv_call(proj_a.reshape(slots, -1, A_SHIFT_W), a_shift[:, None, :], a_wkv,
                                         rwkv_w, RWKV_SLOTS[kind], nb, log_c)
    o_b, ssm_new, conv_new = _gdn_call(proj_b.reshape(slots, -1, B_PROJ_PAD), b_conv, b_ssm,
                                       gdn_w, GDN_SLOTS[kind], nb, log_c)
    y2, fconv_new = _post_call(o_a.reshape(m, A_WIDTH), o_b.reshape(m, B_WIDTH), x2, p2, f_conv, post_w, seq)
    new_states = (wkv_new, shift_new[:, 0, :], ssm_new, conv_new, fconv_new)
    return y2.reshape(bsz, seq, D_MODEL), tuple(s[None] for s in new_states)


def kernel(x_prompt, x_sample, p_prompt, p_sample, state_a_wkv, state_a_shift, state_b_ssm, state_b_conv, state_ffn_conv, w_in, a_mu, a_w0, a_w_w2, a_a0, a_w_a2, a_w_g2, a_k_k, a_k_a, a_r_k, a_gn_g, a_gn_b, b_conv_w, b_a_log, b_dt_bias, b_norm_g, w_o, ln1_g, ln1_b, w_up, f_conv_w, f_conv_b, w_down, ln2_g, ln2_b, w_ple, ple_g, w_ple_gate):
    assert w_in.shape[0] == DEPTH
    weights = _prepare_weights(w_in, a_mu, a_w0, a_w_w2, a_a0, a_w_a2, a_w_g2, a_k_k, a_k_a, a_r_k,
                               a_gn_g, a_gn_b, b_conv_w, b_a_log, b_dt_bias, b_norm_g, w_o, ln1_g,
                               ln1_b, w_up, f_conv_w, f_conv_b, w_down, ln2_g, ln2_b, w_ple, ple_g,
                               w_ple_gate)
    bp = x_prompt.shape[0]
    zeros = lambda *s: jnp.zeros((DEPTH, bp) + s, x_prompt.dtype)
    prompt_init = (zeros(A_HEADS, A_HEAD, A_HEAD), zeros(A_SHIFT_W), zeros(B_HEADS, B_HEAD, B_HEAD),
                   zeros(B_CONV - 1, B_QKV), zeros(FFN_CONV - 1, D_FF))
    y_prompt, prompt_states = _trunk(x_prompt, p_prompt[0], prompt_init, weights)
    sample_init = (state_a_wkv, state_a_shift, state_b_ssm, state_b_conv, state_ffn_conv)
    y_sample, sample_states = _trunk(x_sample, p_sample[0], sample_init, weights)
    return (y_prompt, y_sample) + prompt_states + sample_states
```

```python
import functools

import jax
import jax.numpy as jnp
from jax import lax
from jax.experimental import pallas as pl
from jax.experimental.pallas import tpu as pltpu

F32 = jnp.float32
BF16 = jnp.bfloat16

LANES = 128
SUBLANES = 8
MXU_TILE = 256
VMEM_LIMIT_BYTES = 56 * 1024 * 1024

D_MODEL = 1024
A_WIDTH = 512
A_HEAD = 64
A_HEADS = 8
A_SHIFT_W = 1792
B_WIDTH = 512
B_HEAD = 128
B_HEADS = 4
B_CONV = 4
B_QKV = 1536
B_PROJ_PAD = 2176
D_FF = 2816
FFN_CONV = 3
PLE_DIM = 256
DEPTH = 1
DN_ALPHA = (2.0 * DEPTH) ** 0.25
LN_EPS = 1e-5
GN_EPS = 64e-5
RMS_EPS = 1e-6
L2_EPS = 1e-12

SLOT_ROWS = 64
POST_ROWS = (512, 256)
PROJ_ROWS = 512
FF_CUTS = (0, 1536, 2816)
RWKV_SLOTS = ((8, 4), (2, 2))
GDN_SLOTS = ((8, 8), (2, 2))


def _dot(a, b):
    return jnp.dot(a, b, preferred_element_type=F32)


def _mm(a, b):
    return _dot(a.astype(BF16), b.astype(BF16))


def _split(x, n):
    parts = []
    r = x
    for i in range(n):
        h = r.astype(BF16)
        parts.append(h)
        if i + 1 < n:
            r = r - h.astype(F32)
    return parts


def _mm_exact_lhs(m, x, n):
    out = None
    for part in _split(x, n):
        d = _dot(m, part)
        out = d if out is None else out + d
    return out


def _mm_exact_rhs(x, m, n):
    out = None
    for part in _split(x, n):
        d = _dot(part, m)
        out = d if out is None else out + d
    return out


def _segment_sums(x, log_seg, n_split):
    tile = min(MXU_TILE, x.shape[1])
    same = (jnp.right_shift(_iota((tile, tile), 0), log_seg) == jnp.right_shift(_iota((tile, tile), 1), log_seg))
    ones = jnp.where(same, 1.0, 0.0).astype(BF16)
    parts = [_mm_exact_rhs(x[:, s:s + tile], ones, n_split) for s in range(0, x.shape[1], tile)]
    return parts[0] if len(parts) == 1 else jnp.concatenate(parts, axis=1)


def _sigmoid(x):
    return 1.0 / (1.0 + jnp.exp(-x))


def _silu(x):
    return x * _sigmoid(x)


def _softplus(x):
    return jnp.maximum(x, 0.0) + jnp.log1p(jnp.exp(-jnp.abs(x)))


def _iota(shape, dim):
    return lax.broadcasted_iota(jnp.int32, shape, dim)


def _shift_rows(x, j, prev, c):
    rows = x.shape[0]
    rolled = pltpu.roll(x, j, 0)
    if c == SUBLANES:
        t = jnp.bitwise_and(_iota((rows, 1), 0), SUBLANES - 1)
        return jnp.where(t >= j, rolled, pltpu.roll(prev, rows - SUBLANES + j, 0))
    t = _iota((SUBLANES, 1), 0)
    parts = []
    for i in range(rows // c):
        head = jnp.where(t >= j, rolled[i * c:i * c + SUBLANES],
                         pltpu.roll(prev[i * SUBLANES:(i + 1) * SUBLANES], j, 0))
        parts += [head, rolled[i * c + SUBLANES:(i + 1) * c]]
    return jnp.concatenate(parts, axis=0)


def _load_prev(prev_scr, buf_ref, k_prev):
    prev_scr[...] = jnp.zeros(prev_scr.shape, F32)
    for i in range(buf_ref.shape[0]):
        prev_scr[SUBLANES * (i + 1) - k_prev:SUBLANES * (i + 1), :] = buf_ref[i]


def _store_tails(x, c, k_prev, prev_scr, tail_ref, seq0=0):
    for i in range(x.shape[0] // c):
        q = seq0 + i
        tail_ref[q] = x[(i + 1) * c - k_prev:(i + 1) * c, :]
        if c > SUBLANES:
            prev_scr[q * SUBLANES:(q + 1) * SUBLANES, :] = x[(i + 1) * c - SUBLANES:(i + 1) * c, :]


def _run_waves(n_waves, pre, work):
    def drain(gen):
        while True:
            try:
                next(gen)
            except StopIteration as stop:
                return stop.value

    ready = drain(pre(0))
    for w in range(n_waves):
        cur = work(w, ready)
        nxt = pre(w + 1) if w + 1 < n_waves else None
        cur_done = False
        while not cur_done or nxt is not None:
            if not cur_done:
                try:
                    next(cur)
                except StopIteration:
                    cur_done = True
            if nxt is not None:
                try:
                    next(nxt)
                except StopIteration as stop:
                    ready = stop.value
                    nxt = None


def _seq_cumsum_matrix(rows, log_c):
    r = _iota((rows, rows), 0)
    q = _iota((rows, rows), 1)
    same = jnp.right_shift(r, log_c) == jnp.right_shift(q, log_c)
    return jnp.where(jnp.logical_and(same, r >= q), 1.0, 0.0).astype(BF16)


def _folded_masks(rows, log_c):
    r = _iota((rows, 2 * rows), 0)
    q = jnp.bitwise_and(_iota((rows, 2 * rows), 1), rows - 1)
    same = jnp.right_shift(r, log_c) == jnp.right_shift(q, log_c)
    return jnp.logical_and(same, r > q), jnp.logical_and(same, r >= q)


def _proj_kernel(x_ref, w_ref, pa_ref, pb_ref):
    xb = x_ref[...].astype(BF16)
    in_w = w_ref.shape[1]
    b_w = in_w - A_SHIFT_W
    pa_ref[...] = _dot(xb, w_ref[:, 0:A_SHIFT_W])
    pb_ref[:, 0:b_w] = _dot(xb, w_ref[:, A_SHIFT_W:in_w])
    pb_ref[:, b_w:B_PROJ_PAD] = jnp.zeros((pb_ref.shape[0], B_PROJ_PAD - b_w), F32)


def _const_spec(shape):
    zeros = (0,) * len(shape)
    return pl.BlockSpec(shape, lambda *_: zeros, pipeline_mode=pl.Buffered(1))


def _proj_call(x2, w_in):
    m = x2.shape[0]
    tm = PROJ_ROWS
    return pl.pallas_call(
        _proj_kernel,
        grid=(m // tm,),
        in_specs=[pl.BlockSpec((tm, D_MODEL), lambda i: (i, 0)), _const_spec(w_in.shape)],
        out_specs=[pl.BlockSpec((tm, A_SHIFT_W), lambda i: (i, 0)),
                   pl.BlockSpec((tm, B_PROJ_PAD), lambda i: (i, 0))],
        out_shape=[jax.ShapeDtypeStruct((m, A_SHIFT_W), F32),
                   jax.ShapeDtypeStruct((m, B_PROJ_PAD), F32)],
        compiler_params=pltpu.CompilerParams(
            dimension_semantics=("arbitrary",), vmem_limit_bytes=VMEM_LIMIT_BYTES),
        name="in_proj",
    )(x2, w_in)


def _rwkv_kernel(pa_ref, sh0_ref, s0_ref, mu_ref, vec_ref, ww2_ref, wa2_ref, wg2_ref,
                 o_ref, sfin_ref, shout_ref, s_scr, prev_scr, *, ns, wave, nb, log_c, n_t):
    c = 1 << log_c
    rows = SLOT_ROWS
    n2 = 2 * rows
    pairs = A_HEADS // 2
    n_seq = ns * nb
    t_idx = pl.program_id(1)

    @pl.when(t_idx == 0)
    def _init():
        z = jnp.zeros((A_HEAD, A_HEAD), F32)
        for i in range(n_seq):
            for p in range(pairs):
                top = jnp.concatenate([s0_ref[i, 2 * p], z], axis=1)
                bot = jnp.concatenate([z, s0_ref[i, 2 * p + 1]], axis=1)
                s_scr[i, p] = jnp.concatenate([top, bot], axis=0).T
        _load_prev(prev_scr, sh0_ref, 1)

    w0 = vec_ref[0:1, :]
    a0 = vec_ref[1:2, :]
    k_k = vec_ref[2:3, :]
    k_a = vec_ref[3:4, :]
    r_k = vec_ref[4:5, :]
    gn_g = vec_ref[5:6, :]
    gn_b = vec_ref[6:7, :]

    def head_sum(x, n_split):
        return _segment_sums(x, 6, n_split)

    strict, incl = _folded_masks(rows, log_c)
    lane0 = _iota((1, LANES), 1) < A_HEAD
    col_seq = jnp.bitwise_and(jnp.right_shift(_iota((1, n2), 1), log_c), nb - 1)

    def stack(xp):
        return jnp.concatenate([jnp.where(lane0, xp, 0.0), jnp.where(lane0, 0.0, xp)], axis=0)

    def seq_cols(xt, b):
        return xt if nb == 1 else jnp.where(col_seq == b, xt, 0.0)

    def pre(w):
        slot0 = w * wave
        seq0 = slot0 * nb
        u = pa_ref[slot0:slot0 + wave].reshape(wave * rows, A_SHIFT_W)
        prev = prev_scr[seq0 * SUBLANES:(seq0 + wave * nb) * SUBLANES, :]
        xs = u + (_shift_rows(u, 1, prev, c) - u) * mu_ref[...]
        _store_tails(u, c, 1, prev_scr, shout_ref, seq0)
        r = xs[:, 0:512]
        k = xs[:, 512:1024]
        v = xs[:, 1024:1536]
        wa = xs[:, 1536:1664]
        gd = xs[:, 1664:1792]
        wlora = _dot(jnp.tanh(wa).astype(BF16), ww2_ref[...])
        alora = _dot(wa.astype(BF16), wa2_ref[...])
        g = _dot(_sigmoid(gd).astype(BF16), wg2_ref[...])
        kkr = k * k_k
        kk_ss = head_sum(kkr * kkr, 1)
        yield
        wlog = -_softplus(-(w0 + wlora)) - 0.5
        lw = -jnp.exp(wlog)
        gcum = _mm_exact_lhs(_seq_cumsum_matrix(wave * rows, log_c), lw, 2)
        a = _sigmoid(a0 + alora)
        kk = kkr * lax.rsqrt(kk_ss + L2_EPS)
        k2 = k * (1.0 + (a - 1.0) * k_a)
        bonus = head_sum(r * k2 * r_k, 1) * v
        yield
        e_g = jnp.exp(gcum)
        e_gi = jnp.exp(-gcum)
        a_t = -kk * jnp.exp(gcum - lw)
        b_t = kk * a * e_gi
        k_t = k2 * e_gi
        r_t = r * e_g
        return a_t, b_t, k_t, r_t, v, e_g, bonus, g

    def work(w, prepared):
        a_t, b_t, k_t, r_t, v, e_g, bonus, g = prepared
        slot0 = w * wave
        chains = [(s, p) for s in range(wave) for p in range(pairs)]
        a_l, r_l, v_l, bt_l, kt_l, n_l, ak_l, rbk_l, dec_l = ([] for _ in range(9))
        for s, p in chains:
            rs = slice(s * rows, (s + 1) * rows)
            sl = slice(p * LANES, (p + 1) * LANES)
            bk_t = jnp.concatenate([stack(b_t[rs, sl]), stack(k_t[rs, sl])], axis=0).T
            x = _mm(jnp.concatenate([a_t[rs, sl], r_t[rs, sl]], axis=0), bk_t)
            a_l.append(a_t[rs, sl])
            r_l.append(r_t[rs, sl])
            v_l.append(stack(v[rs, sl]).astype(BF16))
            bt_l.append(bk_t[:, 0:n2])
            kt_l.append(bk_t[:, n2:2 * n2])
            n_l.append(jnp.where(strict, x[0:rows, 0:n2], 0.0))
            ak_l.append(jnp.where(strict, x[0:rows, n2:2 * n2], 0.0))
            rbk_l.append(jnp.concatenate([jnp.where(incl, x[rows:n2, 0:n2], 0.0),
                                          jnp.where(incl, x[rows:n2, n2:2 * n2], 0.0)], axis=1))
            dec_l.append(e_g[rs, sl].T)
        yield
        tm_l = list(n_l)
        pw_l = list(n_l)
        pwb_l = [stack(q).astype(BF16) for q in pw_l]
        for _ in range(log_c - 1):
            pw_l = [_mm(q, qb) for q, qb in zip(pw_l, pwb_l)]
            pwb_l = [stack(q).astype(BF16) for q in pw_l]
            tm_l = [t + q + _mm(t, qb) for t, q, qb in zip(tm_l, pw_l, pwb_l)]
            yield
        akv_l = [_mm(ak, v_s) for ak, v_s in zip(ak_l, v_l)]
        kv_l = [[_mm(seq_cols(kt_l[ci], b), v_l[ci]) for b in range(nb)] for ci in range(len(chains))]
        yield
        wu_l = []
        for ci in range(len(chains)):
            rhs = jnp.concatenate([a_l[ci], akv_l[ci]], axis=1)
            rhs_s = jnp.concatenate([stack(a_l[ci]), stack(akv_l[ci])], axis=1)
            wu_l.append(rhs + _mm(tm_l[ci], rhs_s))
        yield
        sa_l, rs_l = [], []
        for ci, (s, p) in enumerate(chains):
            sa_parts, rs_parts = [], []
            for b in range(nb):
                m_old = s_scr[(slot0 + s) * nb + b, p]
                sa_parts.append(_mm(wu_l[ci][b * c:(b + 1) * c, 0:LANES], m_old)
                                + wu_l[ci][b * c:(b + 1) * c, LANES:2 * LANES])
                rs_parts.append(_mm(r_l[ci][b * c:(b + 1) * c], m_old))
            sa_l.append(sa_parts[0] if nb == 1 else jnp.concatenate(sa_parts, axis=0))
            rs_l.append(rs_parts[0] if nb == 1 else jnp.concatenate(rs_parts, axis=0))
        yield
        sas_l = [stack(sa).astype(BF16) for sa in sa_l]
        for ci, (s, p) in enumerate(chains):
            for b in range(nb):
                q = (slot0 + s) * nb + b
                upd = _mm(seq_cols(bt_l[ci], b), sas_l[ci]) + kv_l[ci][b]
                s_scr[q, p] = (s_scr[q, p] + upd) * dec_l[ci][:, (b + 1) * c - 1:(b + 1) * c]
        o_rows = []
        for s in range(wave):
            o_parts = []
            for p in range(pairs):
                ci = s * pairs + p
                o_parts.append(rs_l[ci] + _mm(rbk_l[ci], jnp.concatenate([sas_l[ci], v_l[ci]], axis=0)))
            o_rows.append(jnp.concatenate(o_parts, axis=1))
        o = jnp.concatenate(o_rows, axis=0)
        yield
        mean = head_sum(o, 2) * (1.0 / A_HEAD)
        d = o - mean
        var = head_sum(d * d, 1) * (1.0 / A_HEAD)
        out = (d * lax.rsqrt(var + GN_EPS) * gn_g + gn_b + bonus) * g
        o_ref[slot0:slot0 + wave] = out.reshape(wave, rows, A_WIDTH)

    _run_waves(ns // wave, pre, work)

    @pl.when(t_idx == n_t - 1)
    def _fin():
        for i in range(n_seq):
            for p in range(pairs):
                s_pair = s_scr[i, p].T
                sfin_ref[i, 2 * p] = s_pair[0:A_HEAD, 0:A_HEAD]
                sfin_ref[i, 2 * p + 1] = s_pair[A_HEAD:LANES, A_HEAD:LANES]


def _mixer_specs(ns, nb, width_in, width_out, state_block, tail_rows, tail_width):
    n_seq = ns * nb
    data = lambda g, t: (g, t, 0)
    grp3 = lambda g, t: (g, 0, 0)
    grp4 = lambda g, t: (g, 0, 0, 0)
    in_specs = [pl.BlockSpec((ns, SLOT_ROWS, width_in), data),
                pl.BlockSpec((n_seq, tail_rows, tail_width), grp3),
                pl.BlockSpec((n_seq,) + state_block, grp4)]
    out_specs = [pl.BlockSpec((ns, SLOT_ROWS, width_out), data),
                 pl.BlockSpec((n_seq,) + state_block, grp4),
                 pl.BlockSpec((n_seq, tail_rows, tail_width), grp3)]
    return in_specs, out_specs


def _rwkv_call(proj_a, shift0, wkv0, wts, ns, wave, nb, log_c):
    n_t = proj_a.shape[1] // SLOT_ROWS
    prev_rows = (SUBLANES if nb == 1 else SLOT_ROWS) * ns
    in_specs, out_specs = _mixer_specs(ns, nb, A_SHIFT_W, A_WIDTH, (A_HEADS, A_HEAD, A_HEAD), 1, A_SHIFT_W)
    kern = functools.partial(_rwkv_kernel, ns=ns, wave=wave, nb=nb, log_c=log_c, n_t=n_t)
    return pl.pallas_call(
        kern,
        grid=(proj_a.shape[0] // ns, n_t),
        in_specs=in_specs + [_const_spec(w.shape) for w in wts],
        out_specs=out_specs,
        out_shape=[jax.ShapeDtypeStruct(proj_a.shape[:2] + (A_WIDTH,), F32),
                   jax.ShapeDtypeStruct(wkv0.shape, F32),
                   jax.ShapeDtypeStruct(shift0.shape, F32)],
        scratch_shapes=[pltpu.VMEM((ns * nb, A_HEADS // 2, LANES, LANES), F32),
                        pltpu.VMEM((prev_rows, A_SHIFT_W), F32)],
        compiler_params=pltpu.CompilerParams(
            dimension_semantics=("arbitrary", "arbitrary"), vmem_limit_bytes=VMEM_LIMIT_BYTES),
        name="rwkv7_mix",
    )(proj_a, shift0, wkv0, *wts)


def _gdn_kernel(pb_ref, cb_ref, s0_ref, cw_ref, hv_ref,
                o_ref, sfin_ref, cout_ref, s_scr, prev_scr, *, ns, wave, nb, log_c, n_t):
    c = 1 << log_c
    rows = SLOT_ROWS
    n2 = 2 * rows
    units = 2 * nb
    npairs = B_HEADS // 2
    t_idx = pl.program_id(1)

    @pl.when(t_idx == 0)
    def _init():
        s_scr[...] = s0_ref[...]
        _load_prev(prev_scr, cb_ref, B_CONV - 1)

    a_log = hv_ref[0:1, :]
    dt_bias = hv_ref[1:2, :]
    norm_g = jnp.concatenate([hv_ref[2:3, :]] * B_HEADS, axis=1)

    strict, incl = _folded_masks(rows, log_c)
    lane0 = _iota((1, LANES), 1) < rows

    def stack(xf):
        return jnp.concatenate([jnp.where(lane0, xf, 0.0), jnp.where(lane0, 0.0, xf)], axis=0)

    row2 = _iota((n2, n2), 0)
    col2 = _iota((n2, n2), 1)
    last_sel_t = row2 == jnp.bitwise_or(col2, c - 1)
    col_unit = jnp.right_shift(_iota((1, n2), 1), log_c)

    def pre(w):
        slot0 = w * wave
        seq0 = slot0 * nb
        pb = pb_ref[slot0:slot0 + wave].reshape(wave * rows, B_PROJ_PAD)
        x = pb[:, 0:B_QKV]
        z = pb[:, B_QKV:B_QKV + B_WIDTH]
        ba = pb[:, B_QKV + B_WIDTH:B_PROJ_PAD]
        prev = prev_scr[seq0 * SUBLANES:(seq0 + wave * nb) * SUBLANES, :]
        conv = (cw_ref[0:1, :] * _shift_rows(x, 3, prev, c) + cw_ref[1:2, :] * _shift_rows(x, 2, prev, c)
                + cw_ref[2:3, :] * _shift_rows(x, 1, prev, c) + cw_ref[3:4, :] * x)
        _store_tails(x, c, B_CONV - 1, prev_scr, cout_ref, seq0)
        beta_t = _sigmoid(ba)
        g_t = -jnp.exp(a_log) * _softplus(ba + dt_bias)
        gcum_t = _mm_exact_lhs(_seq_cumsum_matrix(wave * rows, log_c), g_t, 3)
        yield
        qkv = _silu(conv)
        qk = qkv[:, 0:2 * B_WIDTH]
        qk_ss = _segment_sums(qk * qk, 7, 1)
        yield
        qk = qk * lax.rsqrt(qk_ss + L2_EPS)
        qn = qk[:, 0:B_WIDTH] * (B_HEAD ** -0.5)
        kn = qk[:, B_WIDTH:2 * B_WIDTH]
        zs = _silu(z)
        return qn, kn, qkv[:, 2 * B_WIDTH:3 * B_WIDTH], beta_t, gcum_t, zs

    def state_index(slot, p, unit):
        return slot * nb + unit % nb, 2 * p + unit // nb

    def work(w, prepared):
        qn, kn, vv, beta_t, gcum_t, zs = prepared
        slot0 = w * wave
        chains = [(s, p) for s in range(wave) for p in range(npairs)]
        nl_l, rhs_l, attn_l, qdec_l, kdt_l, gl_l = ([] for _ in range(6))
        for s, p in chains:
            rs = slice(s * rows, (s + 1) * rows)
            hs = (2 * p, 2 * p + 1)
            q_s = jnp.concatenate([qn[rs, h * LANES:(h + 1) * LANES] for h in hs], axis=0)
            k_s = jnp.concatenate([kn[rs, h * LANES:(h + 1) * LANES] for h in hs], axis=0)
            v_s = jnp.concatenate([vv[rs, h * LANES:(h + 1) * LANES] for h in hs], axis=0)
            beta_s = jnp.broadcast_to(jnp.concatenate([beta_t[rs, h:h + 1] for h in hs], axis=0), (n2, LANES))
            g_s = jnp.concatenate([gcum_t[rs, B_HEADS + h:B_HEADS + h + 1] for h in hs], axis=0)
            gi = jnp.broadcast_to(g_s, (n2, n2))
            gj = gi.T
            gi_f = jnp.where(lane0, gi[0:rows], gi[rows:n2])
            dec_f = jnp.where(incl, jnp.exp(gi_f - gj[0:rows]), 0.0)
            kb = k_s * beta_s
            ks_t = k_s.T
            x = _mm(jnp.concatenate([kb, q_s], axis=0), ks_t)
            e_gs = jnp.exp(gi)
            g_last = jnp.sum(jnp.where(last_sel_t, gi, 0.0), axis=0, keepdims=True)
            l_f = jnp.where(lane0, x[0:rows], x[rows:n2])
            nl_l.append(-(l_f * jnp.where(strict, dec_f, 0.0)))
            attn_l.append((x[n2:2 * n2] * stack(dec_f)).astype(BF16))
            rhs_l.append(jnp.concatenate([v_s * beta_s, kb * e_gs], axis=1))
            qdec_l.append(q_s * e_gs)
            kdt_l.append(ks_t * jnp.exp(g_last - gj[0:1, :]))
            gl_l.append(jnp.exp(g_last))
        yield
        tm_l = list(nl_l)
        pw_l = list(nl_l)
        pwb_l = [stack(q).astype(BF16) for q in pw_l]
        for _ in range(log_c - 1):
            pw_l = [_mm(q, qb) for q, qb in zip(pw_l, pwb_l)]
            pwb_l = [stack(q).astype(BF16) for q in pw_l]
            tm_l = [t + q + _mm(t, qb) for t, q, qb in zip(tm_l, pw_l, pwb_l)]
            yield
        uw_l = [rhs + _mm(stack(tm), rhs) for tm, rhs in zip(tm_l, rhs_l)]
        yield
        vn_l, qs_l = [], []
        for ci, (s, p) in enumerate(chains):
            vn_parts, qs_parts = [], []
            for unit in range(units):
                s_old = s_scr[state_index(slot0 + s, p, unit)]
                vn_parts.append(uw_l[ci][unit * c:(unit + 1) * c, 0:LANES]
                                - _mm(uw_l[ci][unit * c:(unit + 1) * c, LANES:2 * LANES], s_old))
                qs_parts.append(_mm(qdec_l[ci][unit * c:(unit + 1) * c], s_old))
            vn_l.append(jnp.concatenate(vn_parts, axis=0).astype(BF16))
            qs_l.append(jnp.concatenate(qs_parts, axis=0))
        yield
        for ci, (s, p) in enumerate(chains):
            for unit in range(units):
                kd_u = jnp.where(col_unit == unit, kdt_l[ci], 0.0)
                idx = state_index(slot0 + s, p, unit)
                s_scr[idx] = s_scr[idx] * gl_l[ci][:, unit * c:unit * c + 1] + _mm(kd_u, vn_l[ci])
        o_rows = []
        for s in range(wave):
            heads = []
            for p in range(npairs):
                ci = s * npairs + p
                o_s = qs_l[ci] + _mm(attn_l[ci], vn_l[ci])
                heads += [o_s[0:rows], o_s[rows:n2]]
            o_rows.append(jnp.concatenate(heads, axis=1))
        o = jnp.concatenate(o_rows, axis=0)
        yield
        mean_sq = _segment_sums(o * o, 7, 1) * (1.0 / B_HEAD)
        out = o * lax.rsqrt(mean_sq + RMS_EPS) * norm_g * zs
        o_ref[slot0:slot0 + wave] = out.reshape(wave, rows, B_WIDTH)

    _run_waves(ns // wave, pre, work)

    @pl.when(t_idx == n_t - 1)
    def _fin():
        sfin_ref[...] = s_scr[...]


def _gdn_call(proj_b, conv0, ssm0, wts, ns, wave, nb, log_c):
    n_t = proj_b.shape[1] // SLOT_ROWS
    prev_rows = (SUBLANES if nb == 1 else SLOT_ROWS) * ns
    in_specs, out_specs = _mixer_specs(ns, nb, B_PROJ_PAD, B_WIDTH, (B_HEADS, B_HEAD, B_HEAD),
                                       B_CONV - 1, B_QKV)
    kern = functools.partial(_gdn_kernel, ns=ns, wave=wave, nb=nb, log_c=log_c, n_t=n_t)
    return pl.pallas_call(
        kern,
        grid=(proj_b.shape[0] // ns, n_t),
        in_specs=in_specs + [_const_spec(w.shape) for w in wts],
        out_specs=out_specs,
        out_shape=[jax.ShapeDtypeStruct(proj_b.shape[:2] + (B_WIDTH,), F32),
                   jax.ShapeDtypeStruct(ssm0.shape, F32),
                   jax.ShapeDtypeStruct(conv0.shape, F32)],
        scratch_shapes=[pltpu.VMEM((ns * nb, B_HEADS, B_HEAD, B_HEAD), F32),
                        pltpu.VMEM((prev_rows, B_QKV), F32)],
        compiler_params=pltpu.CompilerParams(
            dimension_semantics=("arbitrary", "arbitrary"), vmem_limit_bytes=VMEM_LIMIT_BYTES),
        name="gdn_mix",
    )(proj_b, conv0, ssm0, *wts)


def _layer_norm(x, g, b):
    mu = jnp.mean(x, axis=-1, keepdims=True)
    d = x - mu
    var = jnp.mean(d * d, axis=-1, keepdims=True)
    return d * lax.rsqrt(var + LN_EPS) * g + b


def _post_kernel(oa_ref, ob_ref, x_ref, p_ref, fb_ref, woa_ref, wob_ref, lnv_ref, wgu_ref,
                 fv_ref, wd_ref, wple_ref, wpg_ref, y_ref, fout_ref, prev_scr, *, c):
    t_idx = pl.program_id(1)

    @pl.when(t_idx == 0)
    def _init():
        _load_prev(prev_scr, fb_ref, FFN_CONV - 1)

    ln1_g, ln1_b = lnv_ref[0:1, :], lnv_ref[1:2, :]
    ln2_g, ln2_b = lnv_ref[2:3, :], lnv_ref[3:4, :]
    ple_g = lnv_ref[4:5, :]

    mix = _dot(oa_ref[...].astype(BF16), woa_ref[...]) + _dot(ob_ref[...].astype(BF16), wob_ref[...])
    h = _layer_norm(DN_ALPHA * x_ref[...] + mix, ln1_g, ln1_b)
    hb = h.astype(BF16)
    ffn = None
    for f0, f1 in zip(FF_CUTS[:-1], FF_CUTS[1:]):
        fs = slice(f0, f1)
        gate = _dot(hb, wgu_ref[:, fs])
        up = _dot(hb, wgu_ref[:, D_FF + f0:D_FF + f1])
        prev = prev_scr[:, fs]
        gate_c = (fv_ref[0:1, fs] * _shift_rows(gate, 2, prev, c) + fv_ref[1:2, fs] * _shift_rows(gate, 1, prev, c)
                  + fv_ref[2:3, fs] * gate)
        _store_tails(gate, c, FFN_CONV - 1, prev_scr.at[:, fs], fout_ref.at[:, :, fs])
        act = _silu(gate_c + fv_ref[3:4, fs]) * up
        part = _dot(act.astype(BF16), wd_ref[fs, :])
        ffn = part if ffn is None else ffn + part
    h2 = _layer_norm(DN_ALPHA * h + ffn, ln2_g, ln2_b)
    pe = _dot(p_ref[...].astype(BF16), wple_ref[...])
    e = pe * lax.rsqrt(jnp.mean(pe * pe, axis=-1, keepdims=True) + RMS_EPS) * ple_g
    y_ref[...] = h2 + _sigmoid(_dot(h2.astype(BF16), wpg_ref[...])) * e


def _post_call(o_a, o_b, x2, p2, fconv0, wts, seq):
    rows = POST_ROWS[0] if seq % POST_ROWS[0] == 0 else POST_ROWS[1]
    c = rows if seq % rows == 0 else seq
    nb = rows // c
    n_t = seq // c
    groups = fconv0.shape[0] // nb
    prev_rows = SUBLANES if nb == 1 else rows
    row_map = lambda g, t: (g * n_t + t, 0)
    grp3 = lambda g, t: (g, 0, 0)
    kern = functools.partial(_post_kernel, c=c)
    return pl.pallas_call(
        kern,
        grid=(groups, n_t),
        in_specs=[pl.BlockSpec((rows, A_WIDTH), row_map),
                  pl.BlockSpec((rows, B_WIDTH), row_map),
                  pl.BlockSpec((rows, D_MODEL), row_map),
                  pl.BlockSpec((rows, PLE_DIM), row_map),
                  pl.BlockSpec((nb, FFN_CONV - 1, D_FF), grp3)]
                 + [_const_spec(w.shape) for w in wts],
        out_specs=[pl.BlockSpec((rows, D_MODEL), row_map),
                   pl.BlockSpec((nb, FFN_CONV - 1, D_FF), grp3)],
        out_shape=[jax.ShapeDtypeStruct(x2.shape, F32),
                   jax.ShapeDtypeStruct(fconv0.shape, F32)],
        scratch_shapes=[pltpu.VMEM((prev_rows, D_FF), F32)],
        compiler_params=pltpu.CompilerParams(
            dimension_semantics=("arbitrary", "arbitrary"), vmem_limit_bytes=VMEM_LIMIT_BYTES),
        name="post_mix_ffn",
    )(o_a, o_b, x2, p2, fconv0, *wts)


def _pad_rows(w, n):
    return jnp.pad(w, ((0, n - w.shape[0]), (0, 0)))


def _prepare_weights(w_in, a_mu, a_w0, a_w_w2, a_a0, a_w_a2, a_w_g2, a_k_k, a_k_a, a_r_k, a_gn_g,
                     a_gn_b, b_conv_w, b_a_log, b_dt_bias, b_norm_g, w_o, ln1_g, ln1_b, w_up,
                     f_conv_w, f_conv_b, w_down, ln2_g, ln2_b, w_ple, ple_g, w_ple_gate):
    lora = 64
    proj = (w_in[0].astype(BF16),)
    rwkv = (a_mu,
            _pad_rows(jnp.concatenate([a_w0, a_a0, a_k_k, a_k_a, a_r_k, a_gn_g, a_gn_b], axis=0), SUBLANES),
            _pad_rows(a_w_w2[0], LANES).astype(BF16),
            jnp.pad(a_w_a2[0], ((lora, LANES - 2 * lora), (0, 0))).astype(BF16),
            a_w_g2[0].astype(BF16))
    head_row = lambda vec: jnp.pad(vec, ((0, 0), (B_HEADS, LANES - 2 * B_HEADS)))
    gdn = (_pad_rows(b_conv_w[0], SUBLANES),
           _pad_rows(jnp.concatenate([head_row(b_a_log), head_row(b_dt_bias), b_norm_g], axis=0), SUBLANES))
    post = (w_o[0, :A_WIDTH].astype(BF16), w_o[0, A_WIDTH:].astype(BF16),
            _pad_rows(jnp.concatenate([ln1_g, ln1_b, ln2_g, ln2_b, ple_g], axis=0), SUBLANES),
            w_up[0].astype(BF16),
            _pad_rows(jnp.concatenate([f_conv_w[0], f_conv_b], axis=0), SUBLANES),
            w_down[0].astype(BF16), w_ple[0].astype(BF16), w_ple_gate[0].astype(BF16))
    return proj, rwkv, gdn, post


def _trunk(x, p, states, weights):
    proj_w, rwkv_w, gdn_w, post_w = weights
    a_wkv, a_shift, b_ssm, b_conv, f_conv = (s[0] for s in states)
    bsz, seq, _ = x.shape
    m = bsz * seq
    x2 = x.reshape(m, D_MODEL)
    p2 = p.reshape(m, PLE_DIM)
    if seq % SLOT_ROWS == 0:
        nb, log_c, slots, kind = 1, 6, bsz, 0
    else:
        assert seq == SUBLANES
        nb, log_c, slots, kind = SLOT_ROWS // seq, 3, m // SLOT_ROWS, 1
    proj_a, proj_b = _proj_call(x2, *proj_w)
    o_a, wkv_new, shift_new = _rwkv_call(proj_a.reshape(slots, -1, A_SHIFT_W), a_shift[:, None, :], a_wkv,
                                         rwkv_w, *RWKV_SLOTS[kind], nb, log_c)
    o_b, ssm_new, conv_new = _gdn_call(proj_b.reshape(slots, -1, B_PROJ_PAD), b_conv, b_ssm,
                                       gdn_w, *GDN_SLOTS[kind], nb, log_c)
    y2, fconv_new = _post_call(o_a.reshape(m, A_WIDTH), o_b.reshape(m, B_WIDTH), x2, p2, f_conv, post_w, seq)
    new_states = (wkv_new, shift_new[:, 0, :], ssm_new, conv_new, fconv_new)
    return y2.reshape(bsz, seq, D_MODEL), tuple(s[None] for s in new_states)


def kernel(x_prompt, x_sample, p_prompt, p_sample, state_a_wkv, state_a_shift, state_b_ssm, state_b_conv, state_ffn_conv, w_in, a_mu, a_w0, a_w_w2, a_a0, a_w_a2, a_w_g2, a_k_k, a_k_a, a_r_k, a_gn_g, a_gn_b, b_conv_w, b_a_log, b_dt_bias, b_norm_g, w_o, ln1_g, ln1_b, w_up, f_conv_w, f_conv_b, w_down, ln2_g, ln2_b, w_ple, ple_g, w_ple_gate):
    assert w_in.shape[0] == DEPTH
    weights = _prepare_weights(w_in, a_mu, a_w0, a_w_w2, a_a0, a_w_a2, a_w_g2, a_k_k, a_k_a, a_r_k,
                               a_gn_g, a_gn_b, b_conv_w, b_a_log, b_dt_bias, b_norm_g, w_o, ln1_g,
                               ln1_b, w_up, f_conv_w, f_conv_b, w_down, ln2_g, ln2_b, w_ple, ple_g,
                               w_ple_gate)
    bp = x_prompt.shape[0]
    zeros = lambda *s: jnp.zeros((DEPTH, bp) + s, x_prompt.dtype)
    prompt_init = (zeros(A_HEADS, A_HEAD, A_HEAD), zeros(A_SHIFT_W), zeros(B_HEADS, B_HEAD, B_HEAD),
                   zeros(B_CONV - 1, B_QKV), zeros(FFN_CONV - 1, D_FF))
    y_prompt, prompt_states = _trunk(x_prompt, p_prompt[0], prompt_init, weights)
    sample_init = (state_a_wkv, state_a_shift, state_b_ssm, state_b_conv, state_ffn_conv)
    y_sample, sample_states = _trunk(x_sample, p_sample[0], sample_init, weights)
    return (y_prompt, y_sample) + prompt_states + sample_states
```

```python
import functools

import jax
import jax.numpy as jnp
from jax import lax
from jax.experimental import pallas as pl
from jax.experimental.pallas import tpu as pltpu

F32 = jnp.float32
BF16 = jnp.bfloat16

LANES = 128
SUBLANES = 8
MXU_TILE = 256
VMEM_LIMIT_BYTES = 56 * 1024 * 1024

D_MODEL = 1024
A_WIDTH = 512
A_HEAD = 64
A_HEADS = 8
A_SHIFT_W = 1792
B_WIDTH = 512
B_HEAD = 128
B_HEADS = 4
B_CONV = 4
B_QKV = 1536
B_PROJ_PAD = 2176
D_FF = 2816
FFN_CONV = 3
PLE_DIM = 256
DEPTH = 1
DN_ALPHA = (2.0 * DEPTH) ** 0.25
LN_EPS = 1e-5
GN_EPS = 64e-5
RMS_EPS = 1e-6
L2_EPS = 1e-12

SLOT_ROWS = 64
POST_ROWS = (512, 256)
PROJ_ROWS = 512
FF_CUTS = (0, 1536, 2816)
RWKV_SLOTS = ((8, 4), (2, 2))
GDN_SLOTS = ((8, 8), (2, 2))


def _dot(a, b):
    return jnp.dot(a, b, preferred_element_type=F32)


def _mm(a, b):
    return _dot(a.astype(BF16), b.astype(BF16))


def _split(x, n):
    parts = []
    r = x
    for i in range(n):
        h = r.astype(BF16)
        parts.append(h)
        if i + 1 < n:
            r = r - h.astype(F32)
    return parts


def _mm_exact_lhs(m, x, n):
    out = None
    for part in _split(x, n):
        d = _dot(m, part)
        out = d if out is None else out + d
    return out


def _mm_exact_rhs(x, m, n):
    out = None
    for part in _split(x, n):
        d = _dot(part, m)
        out = d if out is None else out + d
    return out


def _segment_sums(x, log_seg, n_split):
    tile = min(MXU_TILE, x.shape[1])
    same = (jnp.right_shift(_iota((tile, tile), 0), log_seg) == jnp.right_shift(_iota((tile, tile), 1), log_seg))
    ones = jnp.where(same, 1.0, 0.0).astype(BF16)
    parts = [_mm_exact_rhs(x[:, s:s + tile], ones, n_split) for s in range(0, x.shape[1], tile)]
    return parts[0] if len(parts) == 1 else jnp.concatenate(parts, axis=1)


def _sigmoid(x):
    return 1.0 / (1.0 + jnp.exp(-x))


def _silu(x):
    return x * _sigmoid(x)


def _softplus(x):
    return jnp.maximum(x, 0.0) + jnp.log1p(jnp.exp(-jnp.abs(x)))


def _iota(shape, dim):
    return lax.broadcasted_iota(jnp.int32, shape, dim)


def _shift_rows(x, j, prev, c):
    rows = x.shape[0]
    rolled = pltpu.roll(x, j, 0)
    if c == SUBLANES:
        t = jnp.bitwise_and(_iota((rows, 1), 0), SUBLANES - 1)
        return jnp.where(t >= j, rolled, pltpu.roll(prev, rows - SUBLANES + j, 0))
    t = _iota((SUBLANES, 1), 0)
    parts = []
    for i in range(rows // c):
        head = jnp.where(t >= j, rolled[i * c:i * c + SUBLANES],
                         pltpu.roll(prev[i * SUBLANES:(i + 1) * SUBLANES], j, 0))
        parts += [head, rolled[i * c + SUBLANES:(i + 1) * c]]
    return jnp.concatenate(parts, axis=0)


def _load_prev(prev_scr, buf_ref, k_prev):
    prev_scr[...] = jnp.zeros(prev_scr.shape, F32)
    for i in range(buf_ref.shape[0]):
        prev_scr[SUBLANES * (i + 1) - k_prev:SUBLANES * (i + 1), :] = buf_ref[i]


def _store_tails(x, c, k_prev, prev_scr, tail_ref, seq0=0):
    for i in range(x.shape[0] // c):
        q = seq0 + i
        tail_ref[q] = x[(i + 1) * c - k_prev:(i + 1) * c, :]
        if c > SUBLANES:
            prev_scr[q * SUBLANES:(q + 1) * SUBLANES, :] = x[(i + 1) * c - SUBLANES:(i + 1) * c, :]


def _run_waves(n_waves, pre, work):
    def drain(gen):
        while True:
            try:
                next(gen)
            except StopIteration as stop:
                return stop.value

    ready = drain(pre(0))
    for w in range(n_waves):
        cur = work(w, ready)
        nxt = pre(w + 1) if w + 1 < n_waves else None
        cur_done = False
        while not cur_done or nxt is not None:
            if not cur_done:
                try:
                    next(cur)
                except StopIteration:
                    cur_done = True
            if nxt is not None:
                try:
                    next(nxt)
                except StopIteration as stop:
                    ready = stop.value
                    nxt = None


def _run_steps(t_idx, n_t, n_waves, pre, work, first_ref, next_ref, prep_scr):
    if not prep_scr:
        _run_waves(n_waves, lambda w: pre(w, next_ref), work)
        return
    cur = lax.rem(t_idx, 2)
    nxt = 1 - cur

    def store(w, values, buf):
        for scr, val in zip(prep_scr, values):
            n = val.shape[0]
            scr[buf, w * n:(w + 1) * n, :] = val

    def load(w):
        n = prep_scr[0].shape[1] // n_waves
        return tuple(scr[cur, w * n:(w + 1) * n, :] for scr in prep_scr)

    @pl.when(t_idx == 0)
    def _prologue():
        for w in range(n_waves):
            gen = pre(w, first_ref)
            while True:
                try:
                    next(gen)
                except StopIteration as stop:
                    store(w, stop.value, 0)
                    break

    def pre_all():
        for w in range(n_waves):
            values = yield from pre(w, next_ref)
            store(w, values, nxt)
            yield

    def work_all():
        for w in range(n_waves):
            yield from work(w, load(w))

    gens = [work_all(), pre_all()]
    while gens:
        for gen in list(gens):
            try:
                next(gen)
            except StopIteration:
                gens.remove(gen)


def _seq_cumsum_matrix(rows, log_c):
    r = _iota((rows, rows), 0)
    q = _iota((rows, rows), 1)
    same = jnp.right_shift(r, log_c) == jnp.right_shift(q, log_c)
    return jnp.where(jnp.logical_and(same, r >= q), 1.0, 0.0).astype(BF16)


def _folded_masks(rows, log_c):
    r = _iota((rows, 2 * rows), 0)
    q = jnp.bitwise_and(_iota((rows, 2 * rows), 1), rows - 1)
    same = jnp.right_shift(r, log_c) == jnp.right_shift(q, log_c)
    return jnp.logical_and(same, r > q), jnp.logical_and(same, r >= q)


def _proj_kernel(x_ref, w_ref, cb_ref, cw_ref, pa_ref, pb_ref, cout_ref, prev_scr, *, c):
    t_idx = pl.program_id(1)

    @pl.when(t_idx == 0)
    def _init():
        _load_prev(prev_scr, cb_ref, B_CONV - 1)

    xb = x_ref[...].astype(BF16)
    in_w = w_ref.shape[1]
    gates_w = in_w - A_SHIFT_W - B_QKV - B_WIDTH
    pa_ref[...] = _dot(xb, w_ref[:, 0:A_SHIFT_W])
    pb = _dot(xb, w_ref[:, A_SHIFT_W:in_w])
    x = pb[:, 0:B_QKV]
    prev = prev_scr[...]
    conv = (cw_ref[0:1, :] * _shift_rows(x, 3, prev, c) + cw_ref[1:2, :] * _shift_rows(x, 2, prev, c)
            + cw_ref[2:3, :] * _shift_rows(x, 1, prev, c) + cw_ref[3:4, :] * x)
    _store_tails(x, c, B_CONV - 1, prev_scr, cout_ref)
    qkv = _silu(conv)
    qk = qkv[:, 0:2 * B_WIDTH]
    qk = qk * lax.rsqrt(_segment_sums(qk * qk, 7, 1) + L2_EPS)
    pb_ref[:, 0:B_WIDTH] = qk[:, 0:B_WIDTH] * (B_HEAD ** -0.5)
    pb_ref[:, B_WIDTH:2 * B_WIDTH] = qk[:, B_WIDTH:2 * B_WIDTH]
    pb_ref[:, 2 * B_WIDTH:B_QKV] = qkv[:, 2 * B_WIDTH:B_QKV]
    pb_ref[:, B_QKV:B_QKV + B_WIDTH] = _silu(pb[:, B_QKV:B_QKV + B_WIDTH])
    pb_ref[:, B_QKV + B_WIDTH:B_QKV + B_WIDTH + gates_w] = pb[:, B_QKV + B_WIDTH:B_QKV + B_WIDTH + gates_w]
    pb_ref[:, B_QKV + B_WIDTH + gates_w:B_PROJ_PAD] = jnp.zeros(
        (pb_ref.shape[0], B_PROJ_PAD - B_QKV - B_WIDTH - gates_w), F32)


def _const_spec(shape):
    zeros = (0,) * len(shape)
    return pl.BlockSpec(shape, lambda *_: zeros, pipeline_mode=pl.Buffered(1))


def _seq_blocking(rows, seq, n_seq):
    c = rows if seq % rows == 0 else seq
    nb = rows // c
    return c, nb, seq // c, n_seq // nb


def _proj_call(x2, conv0, w_in, conv_w, seq):
    m = x2.shape[0]
    rows = PROJ_ROWS
    c, nb, n_t, groups = _seq_blocking(rows, seq, conv0.shape[0])
    prev_rows = SUBLANES if nb == 1 else rows
    row_map = lambda g, t: (g * n_t + t, 0)
    grp3 = lambda g, t: (g, 0, 0)
    return pl.pallas_call(
        functools.partial(_proj_kernel, c=c),
        grid=(groups, n_t),
        in_specs=[pl.BlockSpec((rows, D_MODEL), row_map),
                  _const_spec(w_in.shape),
                  pl.BlockSpec((nb, B_CONV - 1, B_QKV), grp3),
                  _const_spec(conv_w.shape)],
        out_specs=[pl.BlockSpec((rows, A_SHIFT_W), row_map),
                   pl.BlockSpec((rows, B_PROJ_PAD), row_map),
                   pl.BlockSpec((nb, B_CONV - 1, B_QKV), grp3)],
        out_shape=[jax.ShapeDtypeStruct((m, A_SHIFT_W), F32),
                   jax.ShapeDtypeStruct((m, B_PROJ_PAD), F32),
                   jax.ShapeDtypeStruct(conv0.shape, F32)],
        scratch_shapes=[pltpu.VMEM((prev_rows, B_QKV), F32)],
        compiler_params=pltpu.CompilerParams(
            dimension_semantics=("arbitrary", "arbitrary"), vmem_limit_bytes=VMEM_LIMIT_BYTES),
        name="in_proj",
    )(x2, w_in, conv0, conv_w)


def _rwkv_kernel(pa0_ref, pa_ref, s0_ref, sh0_ref, mu_ref, vec_ref, ww2_ref, wa2_ref, wg2_ref,
                 o_ref, sfin_ref, shout_ref, s_scr, prev_scr, *prep_scr, ns, wave, nb, log_c, n_t):
    c = 1 << log_c
    rows = SLOT_ROWS
    n2 = 2 * rows
    pairs = A_HEADS // 2
    n_seq = ns * nb
    t_idx = pl.program_id(1)

    @pl.when(t_idx == 0)
    def _init():
        z = jnp.zeros((A_HEAD, A_HEAD), F32)
        for i in range(n_seq):
            for p in range(pairs):
                top = jnp.concatenate([s0_ref[i, 2 * p], z], axis=1)
                bot = jnp.concatenate([z, s0_ref[i, 2 * p + 1]], axis=1)
                s_scr[i, p] = jnp.concatenate([top, bot], axis=0).T
        _load_prev(prev_scr, sh0_ref, 1)

    w0 = vec_ref[0:1, :]
    a0 = vec_ref[1:2, :]
    k_k = vec_ref[2:3, :]
    k_a = vec_ref[3:4, :]
    r_k = vec_ref[4:5, :]
    gn_g = vec_ref[5:6, :]
    gn_b = vec_ref[6:7, :]

    def head_sum(x, n_split):
        return _segment_sums(x, 6, n_split)

    strict, incl = _folded_masks(rows, log_c)
    lane0 = _iota((1, LANES), 1) < A_HEAD
    col_seq = jnp.bitwise_and(jnp.right_shift(_iota((1, n2), 1), log_c), nb - 1)

    def stack(xp):
        return jnp.concatenate([jnp.where(lane0, xp, 0.0), jnp.where(lane0, 0.0, xp)], axis=0)

    def seq_cols(xt, b):
        return xt if nb == 1 else jnp.where(col_seq == b, xt, 0.0)

    def pre(w, src_ref):
        slot0 = w * wave
        seq0 = slot0 * nb
        u = src_ref[slot0:slot0 + wave].reshape(wave * rows, A_SHIFT_W)
        prev = prev_scr[seq0 * SUBLANES:(seq0 + wave * nb) * SUBLANES, :]
        xs = u + (_shift_rows(u, 1, prev, c) - u) * mu_ref[...]
        _store_tails(u, c, 1, prev_scr, shout_ref, seq0)
        r = xs[:, 0:512]
        k = xs[:, 512:1024]
        v = xs[:, 1024:1536]
        wa = xs[:, 1536:1664]
        gd = xs[:, 1664:1792]
        wlora = _dot(jnp.tanh(wa).astype(BF16), ww2_ref[...])
        alora = _dot(wa.astype(BF16), wa2_ref[...])
        g = _dot(_sigmoid(gd).astype(BF16), wg2_ref[...])
        kkr = k * k_k
        kk_ss = head_sum(kkr * kkr, 1)
        yield
        wlog = -_softplus(-(w0 + wlora)) - 0.5
        lw = -jnp.exp(wlog)
        gcum = _mm_exact_lhs(_seq_cumsum_matrix(wave * rows, log_c), lw, 2)
        a = _sigmoid(a0 + alora)
        kk = kkr * lax.rsqrt(kk_ss + L2_EPS)
        k2 = k * (1.0 + (a - 1.0) * k_a)
        bonus = head_sum(r * k2 * r_k, 1) * v
        yield
        e_g = jnp.exp(gcum)
        e_gi = jnp.exp(-gcum)
        a_t = -kk * jnp.exp(gcum - lw)
        b_t = kk * a * e_gi
        k_t = k2 * e_gi
        r_t = r * e_g
        return a_t, b_t, k_t, r_t, v, e_g, bonus, g

    def work(w, prepared):
        a_t, b_t, k_t, r_t, v, e_g, bonus, g = prepared
        slot0 = w * wave
        chains = [(s, p) for s in range(wave) for p in range(pairs)]
        a_l, r_l, v_l, bt_l, kt_l, n_l, ak_l, rbk_l, dec_l = ([] for _ in range(9))
        for s, p in chains:
            rs = slice(s * rows, (s + 1) * rows)
            sl = slice(p * LANES, (p + 1) * LANES)
            bk_t = jnp.concatenate([stack(b_t[rs, sl]), stack(k_t[rs, sl])], axis=0).T
            x = _mm(jnp.concatenate([a_t[rs, sl], r_t[rs, sl]], axis=0), bk_t)
            a_l.append(a_t[rs, sl])
            r_l.append(r_t[rs, sl])
            v_l.append(stack(v[rs, sl]).astype(BF16))
            bt_l.append(bk_t[:, 0:n2])
            kt_l.append(bk_t[:, n2:2 * n2])
            n_l.append(jnp.where(strict, x[0:rows, 0:n2], 0.0))
            ak_l.append(jnp.where(strict, x[0:rows, n2:2 * n2], 0.0))
            rbk_l.append(jnp.concatenate([jnp.where(incl, x[rows:n2, 0:n2], 0.0),
                                          jnp.where(incl, x[rows:n2, n2:2 * n2], 0.0)], axis=1))
            dec_l.append(e_g[rs, sl].T)
        yield
        tm_l = list(n_l)
        pw_l = list(n_l)
        pwb_l = [stack(q).astype(BF16) for q in pw_l]
        for _ in range(log_c - 1):
            pw_l = [_mm(q, qb) for q, qb in zip(pw_l, pwb_l)]
            pwb_l = [stack(q).astype(BF16) for q in pw_l]
            tm_l = [t + q + _mm(t, qb) for t, q, qb in zip(tm_l, pw_l, pwb_l)]
            yield
        akv_l = [_mm(ak, v_s) for ak, v_s in zip(ak_l, v_l)]
        kv_l = [[_mm(seq_cols(kt_l[ci], b), v_l[ci]) for b in range(nb)] for ci in range(len(chains))]
        yield
        wu_l = []
        for ci in range(len(chains)):
            rhs = jnp.concatenate([a_l[ci], akv_l[ci]], axis=1)
            rhs_s = jnp.concatenate([stack(a_l[ci]), stack(akv_l[ci])], axis=1)
            wu_l.append(rhs + _mm(tm_l[ci], rhs_s))
        yield
        sa_l, rs_l = [], []
        for ci, (s, p) in enumerate(chains):
            sa_parts, rs_parts = [], []
            for b in range(nb):
                m_old = s_scr[(slot0 + s) * nb + b, p]
                sa_parts.append(_mm(wu_l[ci][b * c:(b + 1) * c, 0:LANES], m_old)
                                + wu_l[ci][b * c:(b + 1) * c, LANES:2 * LANES])
                rs_parts.append(_mm(r_l[ci][b * c:(b + 1) * c], m_old))
            sa_l.append(sa_parts[0] if nb == 1 else jnp.concatenate(sa_parts, axis=0))
            rs_l.append(rs_parts[0] if nb == 1 else jnp.concatenate(rs_parts, axis=0))
        yield
        sas_l = [stack(sa).astype(BF16) for sa in sa_l]
        for ci, (s, p) in enumerate(chains):
            for b in range(nb):
                q = (slot0 + s) * nb + b
                upd = _mm(seq_cols(bt_l[ci], b), sas_l[ci]) + kv_l[ci][b]
                s_scr[q, p] = (s_scr[q, p] + upd) * dec_l[ci][:, (b + 1) * c - 1:(b + 1) * c]
        o_rows = []
        for s in range(wave):
            o_parts = []
            for p in range(pairs):
                ci = s * pairs + p
                o_parts.append(rs_l[ci] + _mm(rbk_l[ci], jnp.concatenate([sas_l[ci], v_l[ci]], axis=0)))
            o_rows.append(jnp.concatenate(o_parts, axis=1))
        o = jnp.concatenate(o_rows, axis=0)
        yield
        mean = head_sum(o, 2) * (1.0 / A_HEAD)
        d = o - mean
        var = head_sum(d * d, 1) * (1.0 / A_HEAD)
        out = (d * lax.rsqrt(var + GN_EPS) * gn_g + gn_b + bonus) * g
        o_ref[slot0:slot0 + wave] = out.reshape(wave, rows, A_WIDTH)

    _run_steps(t_idx, n_t, ns // wave, pre, work, pa0_ref, pa_ref, prep_scr)

    @pl.when(t_idx == n_t - 1)
    def _fin():
        for i in range(n_seq):
            for p in range(pairs):
                s_pair = s_scr[i, p].T
                sfin_ref[i, 2 * p] = s_pair[0:A_HEAD, 0:A_HEAD]
                sfin_ref[i, 2 * p + 1] = s_pair[A_HEAD:LANES, A_HEAD:LANES]


def _mixer_specs(ns, nb, n_t, pipelined, width_in, width_out, state_block):
    n_seq = ns * nb
    data = lambda g, t: (g, t, 0)
    data_next = (lambda g, t: (g, jnp.minimum(t + 1, n_t - 1), 0)) if pipelined else data
    grp3 = lambda g, t: (g, 0, 0)
    grp4 = lambda g, t: (g, 0, 0, 0)
    in_specs = [pl.BlockSpec((ns, SLOT_ROWS, width_in), grp3),
                pl.BlockSpec((ns, SLOT_ROWS, width_in), data_next),
                pl.BlockSpec((n_seq,) + state_block, grp4)]
    out_specs = [pl.BlockSpec((ns, SLOT_ROWS, width_out), data),
                 pl.BlockSpec((n_seq,) + state_block, grp4)]
    return in_specs, out_specs


def _rwkv_call(proj_a, shift0, wkv0, wts, ns, wave, nb, log_c):
    n_t = proj_a.shape[1] // SLOT_ROWS
    prev_rows = (SUBLANES if nb == 1 else SLOT_ROWS) * ns
    pipelined = n_t > 1
    in_specs, out_specs = _mixer_specs(ns, nb, n_t, pipelined, A_SHIFT_W, A_WIDTH, (A_HEADS, A_HEAD, A_HEAD))
    shift_spec = pl.BlockSpec((ns * nb, 1, A_SHIFT_W), lambda g, t: (g, 0, 0))
    prep = [pltpu.VMEM((2, ns * SLOT_ROWS, A_WIDTH), F32)] * 8 if pipelined else []
    kern = functools.partial(_rwkv_kernel, ns=ns, wave=wave, nb=nb, log_c=log_c, n_t=n_t)
    return pl.pallas_call(
        kern,
        grid=(proj_a.shape[0] // ns, n_t),
        in_specs=in_specs + [shift_spec] + [_const_spec(w.shape) for w in wts],
        out_specs=out_specs + [shift_spec],
        out_shape=[jax.ShapeDtypeStruct(proj_a.shape[:2] + (A_WIDTH,), F32),
                   jax.ShapeDtypeStruct(wkv0.shape, F32),
                   jax.ShapeDtypeStruct(shift0.shape, F32)],
        scratch_shapes=[pltpu.VMEM((ns * nb, A_HEADS // 2, LANES, LANES), F32),
                        pltpu.VMEM((prev_rows, A_SHIFT_W), F32)] + prep,
        compiler_params=pltpu.CompilerParams(
            dimension_semantics=("arbitrary", "arbitrary"), vmem_limit_bytes=VMEM_LIMIT_BYTES),
        name="rwkv7_mix",
    )(proj_a, proj_a, wkv0, shift0, *wts)


def _gdn_kernel(pb0_ref, pb_ref, s0_ref, hv_ref, o_ref, sfin_ref, s_scr, *prep_scr, ns, wave, nb, log_c, n_t):
    c = 1 << log_c
    rows = SLOT_ROWS
    n2 = 2 * rows
    units = 2 * nb
    npairs = B_HEADS // 2
    t_idx = pl.program_id(1)

    @pl.when(t_idx == 0)
    def _init():
        s_scr[...] = s0_ref[...]

    a_log = hv_ref[0:1, :]
    dt_bias = hv_ref[1:2, :]
    norm_g = jnp.concatenate([hv_ref[2:3, :]] * B_HEADS, axis=1)

    strict, incl = _folded_masks(rows, log_c)
    lane0 = _iota((1, LANES), 1) < rows

    def stack(xf):
        return jnp.concatenate([jnp.where(lane0, xf, 0.0), jnp.where(lane0, 0.0, xf)], axis=0)

    row2 = _iota((n2, n2), 0)
    col2 = _iota((n2, n2), 1)
    last_sel_t = row2 == jnp.bitwise_or(col2, c - 1)
    col_unit = jnp.right_shift(_iota((1, n2), 1), log_c)

    def pre(w, src_ref):
        slot0 = w * wave
        pb = src_ref[slot0:slot0 + wave].reshape(wave * rows, B_PROJ_PAD)
        ba = pb[:, B_QKV + B_WIDTH:B_PROJ_PAD]
        beta_t = _sigmoid(ba)
        g_t = -jnp.exp(a_log) * _softplus(ba + dt_bias)
        gcum_t = _mm_exact_lhs(_seq_cumsum_matrix(wave * rows, log_c), g_t, 3)
        yield
        return (pb[:, 0:B_WIDTH], pb[:, B_WIDTH:2 * B_WIDTH], pb[:, 2 * B_WIDTH:B_QKV], beta_t, gcum_t,
                pb[:, B_QKV:B_QKV + B_WIDTH])

    def state_index(slot, p, unit):
        return slot * nb + unit % nb, 2 * p + unit // nb

    def work(w, prepared):
        qn, kn, vv, beta_t, gcum_t, zs = prepared
        slot0 = w * wave
        chains = [(s, p) for s in range(wave) for p in range(npairs)]
        nl_l, rhs_l, attn_l, qdec_l, kdt_l, gl_l = ([] for _ in range(6))
        for s, p in chains:
            rs = slice(s * rows, (s + 1) * rows)
            hs = (2 * p, 2 * p + 1)
            q_s = jnp.concatenate([qn[rs, h * LANES:(h + 1) * LANES] for h in hs], axis=0)
            k_s = jnp.concatenate([kn[rs, h * LANES:(h + 1) * LANES] for h in hs], axis=0)
            v_s = jnp.concatenate([vv[rs, h * LANES:(h + 1) * LANES] for h in hs], axis=0)
            beta_s = jnp.broadcast_to(jnp.concatenate([beta_t[rs, h:h + 1] for h in hs], axis=0), (n2, LANES))
            g_s = jnp.concatenate([gcum_t[rs, B_HEADS + h:B_HEADS + h + 1] for h in hs], axis=0)
            gi = jnp.broadcast_to(g_s, (n2, n2))
            gj = gi.T
            gi_f = jnp.where(lane0, gi[0:rows], gi[rows:n2])
            dec_f = jnp.where(incl, jnp.exp(gi_f - gj[0:rows]), 0.0)
            kb = k_s * beta_s
            ks_t = k_s.T
            x = _mm(jnp.concatenate([kb, q_s], axis=0), ks_t)
            e_gs = jnp.exp(gi)
            g_last = jnp.sum(jnp.where(last_sel_t, gi, 0.0), axis=0, keepdims=True)
            l_f = jnp.where(lane0, x[0:rows], x[rows:n2])
            nl_l.append(-(l_f * jnp.where(strict, dec_f, 0.0)))
            attn_l.append((x[n2:2 * n2] * stack(dec_f)).astype(BF16))
            rhs_l.append(jnp.concatenate([v_s * beta_s, kb * e_gs], axis=1))
            qdec_l.append(q_s * e_gs)
            kdt_l.append(ks_t * jnp.exp(g_last - gj[0:1, :]))
            gl_l.append(jnp.exp(g_last))
        yield
        tm_l = list(nl_l)
        pw_l = list(nl_l)
        pwb_l = [stack(q).astype(BF16) for q in pw_l]
        for _ in range(log_c - 1):
            pw_l = [_mm(q, qb) for q, qb in zip(pw_l, pwb_l)]
            pwb_l = [stack(q).astype(BF16) for q in pw_l]
            tm_l = [t + q + _mm(t, qb) for t, q, qb in zip(tm_l, pw_l, pwb_l)]
            yield
        uw_l = [rhs + _mm(stack(tm), rhs) for tm, rhs in zip(tm_l, rhs_l)]
        yield
        vn_l, qs_l = [], []
        for ci, (s, p) in enumerate(chains):
            vn_parts, qs_parts = [], []
            for unit in range(units):
                s_old = s_scr[state_index(slot0 + s, p, unit)]
                vn_parts.append(uw_l[ci][unit * c:(unit + 1) * c, 0:LANES]
                                - _mm(uw_l[ci][unit * c:(unit + 1) * c, LANES:2 * LANES], s_old))
                qs_parts.append(_mm(qdec_l[ci][unit * c:(unit + 1) * c], s_old))
            vn_l.append(jnp.concatenate(vn_parts, axis=0).astype(BF16))
            qs_l.append(jnp.concatenate(qs_parts, axis=0))
        yield
        for ci, (s, p) in enumerate(chains):
            for unit in range(units):
                kd_u = jnp.where(col_unit == unit, kdt_l[ci], 0.0)
                idx = state_index(slot0 + s, p, unit)
                s_scr[idx] = s_scr[idx] * gl_l[ci][:, unit * c:unit * c + 1] + _mm(kd_u, vn_l[ci])
        o_rows = []
        for s in range(wave):
            heads = []
            for p in range(npairs):
                ci = s * npairs + p
                o_s = qs_l[ci] + _mm(attn_l[ci], vn_l[ci])
                heads += [o_s[0:rows], o_s[rows:n2]]
            o_rows.append(jnp.concatenate(heads, axis=1))
        o = jnp.concatenate(o_rows, axis=0)
        yield
        mean_sq = _segment_sums(o * o, 7, 1) * (1.0 / B_HEAD)
        out = o * lax.rsqrt(mean_sq + RMS_EPS) * norm_g * zs
        o_ref[slot0:slot0 + wave] = out.reshape(wave, rows, B_WIDTH)

    _run_steps(t_idx, n_t, ns // wave, pre, work, pb0_ref, pb_ref, prep_scr)

    @pl.when(t_idx == n_t - 1)
    def _fin():
        sfin_ref[...] = s_scr[...]


def _gdn_call(proj_b, ssm0, wts, ns, wave, nb, log_c):
    n_t = proj_b.shape[1] // SLOT_ROWS
    in_specs, out_specs = _mixer_specs(ns, nb, n_t, False, B_PROJ_PAD, B_WIDTH, (B_HEADS, B_HEAD, B_HEAD))
    kern = functools.partial(_gdn_kernel, ns=ns, wave=wave, nb=nb, log_c=log_c, n_t=n_t)
    return pl.pallas_call(
        kern,
        grid=(proj_b.shape[0] // ns, n_t),
        in_specs=in_specs + [_const_spec(w.shape) for w in wts],
        out_specs=out_specs,
        out_shape=[jax.ShapeDtypeStruct(proj_b.shape[:2] + (B_WIDTH,), F32),
                   jax.ShapeDtypeStruct(ssm0.shape, F32)],
        scratch_shapes=[pltpu.VMEM((ns * nb, B_HEADS, B_HEAD, B_HEAD), F32)],
        compiler_params=pltpu.CompilerParams(
            dimension_semantics=("arbitrary", "arbitrary"), vmem_limit_bytes=VMEM_LIMIT_BYTES),
        name="gdn_mix",
    )(proj_b, proj_b, ssm0, *wts)


def _layer_norm(x, g, b):
    mu = jnp.mean(x, axis=-1, keepdims=True)
    d = x - mu
    var = jnp.mean(d * d, axis=-1, keepdims=True)
    return d * lax.rsqrt(var + LN_EPS) * g + b


def _post_kernel(oa_ref, ob_ref, x_ref, p_ref, fb_ref, woa_ref, wob_ref, lnv_ref, wgu_ref,
                 fv_ref, wd_ref, wple_ref, wpg_ref, y_ref, fout_ref, prev_scr, *, c):
    t_idx = pl.program_id(1)

    @pl.when(t_idx == 0)
    def _init():
        _load_prev(prev_scr, fb_ref, FFN_CONV - 1)

    ln1_g, ln1_b = lnv_ref[0:1, :], lnv_ref[1:2, :]
    ln2_g, ln2_b = lnv_ref[2:3, :], lnv_ref[3:4, :]
    ple_g = lnv_ref[4:5, :]

    mix = _dot(oa_ref[...].astype(BF16), woa_ref[...]) + _dot(ob_ref[...].astype(BF16), wob_ref[...])
    h = _layer_norm(DN_ALPHA * x_ref[...] + mix, ln1_g, ln1_b)
    hb = h.astype(BF16)
    ffn = None
    for f0, f1 in zip(FF_CUTS[:-1], FF_CUTS[1:]):
        fs = slice(f0, f1)
        gate = _dot(hb, wgu_ref[:, fs])
        up = _dot(hb, wgu_ref[:, D_FF + f0:D_FF + f1])
        prev = prev_scr[:, fs]
        gate_c = (fv_ref[0:1, fs] * _shift_rows(gate, 2, prev, c) + fv_ref[1:2, fs] * _shift_rows(gate, 1, prev, c)
                  + fv_ref[2:3, fs] * gate)
        _store_tails(gate, c, FFN_CONV - 1, prev_scr.at[:, fs], fout_ref.at[:, :, fs])
        act = _silu(gate_c + fv_ref[3:4, fs]) * up
        part = _dot(act.astype(BF16), wd_ref[fs, :])
        ffn = part if ffn is None else ffn + part
    h2 = _layer_norm(DN_ALPHA * h + ffn, ln2_g, ln2_b)
    pe = _dot(p_ref[...].astype(BF16), wple_ref[...])
    e = pe * lax.rsqrt(jnp.mean(pe * pe, axis=-1, keepdims=True) + RMS_EPS) * ple_g
    y_ref[...] = h2 + _sigmoid(_dot(h2.astype(BF16), wpg_ref[...])) * e


def _post_call(o_a, o_b, x2, p2, fconv0, wts, seq):
    rows = POST_ROWS[0] if seq % POST_ROWS[0] == 0 else POST_ROWS[1]
    c, nb, n_t, groups = _seq_blocking(rows, seq, fconv0.shape[0])
    prev_rows = SUBLANES if nb == 1 else rows
    row_map = lambda g, t: (g * n_t + t, 0)
    grp3 = lambda g, t: (g, 0, 0)
    kern = functools.partial(_post_kernel, c=c)
    return pl.pallas_call(
        kern,
        grid=(groups, n_t),
        in_specs=[pl.BlockSpec((rows, A_WIDTH), row_map),
                  pl.BlockSpec((rows, B_WIDTH), row_map),
                  pl.BlockSpec((rows, D_MODEL), row_map),
                  pl.BlockSpec((rows, PLE_DIM), row_map),
                  pl.BlockSpec((nb, FFN_CONV - 1, D_FF), grp3)]
                 + [_const_spec(w.shape) for w in wts],
        out_specs=[pl.BlockSpec((rows, D_MODEL), row_map),
                   pl.BlockSpec((nb, FFN_CONV - 1, D_FF), grp3)],
        out_shape=[jax.ShapeDtypeStruct(x2.shape, F32),
                   jax.ShapeDtypeStruct(fconv0.shape, F32)],
        scratch_shapes=[pltpu.VMEM((prev_rows, D_FF), F32)],
        compiler_params=pltpu.CompilerParams(
            dimension_semantics=("arbitrary", "arbitrary"), vmem_limit_bytes=VMEM_LIMIT_BYTES),
        name="post_mix_ffn",
    )(o_a, o_b, x2, p2, fconv0, *wts)


def _pad_rows(w, n):
    return jnp.pad(w, ((0, n - w.shape[0]), (0, 0)))


def _prepare_weights(w_in, a_mu, a_w0, a_w_w2, a_a0, a_w_a2, a_w_g2, a_k_k, a_k_a, a_r_k, a_gn_g,
                     a_gn_b, b_conv_w, b_a_log, b_dt_bias, b_norm_g, w_o, ln1_g, ln1_b, w_up,
                     f_conv_w, f_conv_b, w_down, ln2_g, ln2_b, w_ple, ple_g, w_ple_gate):
    lora = 64
    proj = (w_in[0].astype(BF16), _pad_rows(b_conv_w[0], SUBLANES))
    rwkv = (a_mu,
            _pad_rows(jnp.concatenate([a_w0, a_a0, a_k_k, a_k_a, a_r_k, a_gn_g, a_gn_b], axis=0), SUBLANES),
            _pad_rows(a_w_w2[0], LANES).astype(BF16),
            jnp.pad(a_w_a2[0], ((lora, LANES - 2 * lora), (0, 0))).astype(BF16),
            a_w_g2[0].astype(BF16))
    head_row = lambda vec: jnp.pad(vec, ((0, 0), (B_HEADS, LANES - 2 * B_HEADS)))
    gdn = (_pad_rows(jnp.concatenate([head_row(b_a_log), head_row(b_dt_bias), b_norm_g], axis=0), SUBLANES),)
    post = (w_o[0, :A_WIDTH].astype(BF16), w_o[0, A_WIDTH:].astype(BF16),
            _pad_rows(jnp.concatenate([ln1_g, ln1_b, ln2_g, ln2_b, ple_g], axis=0), SUBLANES),
            w_up[0].astype(BF16),
            _pad_rows(jnp.concatenate([f_conv_w[0], f_conv_b], axis=0), SUBLANES),
            w_down[0].astype(BF16), w_ple[0].astype(BF16), w_ple_gate[0].astype(BF16))
    return proj, rwkv, gdn, post


def _trunk(x, p, states, weights):
    proj_w, rwkv_w, gdn_w, post_w = weights
    a_wkv, a_shift, b_ssm, b_conv, f_conv = (s[0] for s in states)
    bsz, seq, _ = x.shape
    m = bsz * seq
    x2 = x.reshape(m, D_MODEL)
    p2 = p.reshape(m, PLE_DIM)
    if seq % SLOT_ROWS == 0:
        nb, log_c, slots, kind = 1, 6, bsz, 0
    else:
        assert seq == SUBLANES
        nb, log_c, slots, kind = SLOT_ROWS // seq, 3, m // SLOT_ROWS, 1
    proj_a, proj_b, conv_new = _proj_call(x2, b_conv, *proj_w, seq)
    o_a, wkv_new, shift_new = _rwkv_call(proj_a.reshape(slots, -1, A_SHIFT_W), a_shift[:, None, :], a_wkv,
                                         rwkv_w, *RWKV_SLOTS[kind], nb, log_c)
    o_b, ssm_new = _gdn_call(proj_b.reshape(slots, -1, B_PROJ_PAD), b_ssm, gdn_w, *GDN_SLOTS[kind], nb, log_c)
    y2, fconv_new = _post_call(o_a.reshape(m, A_WIDTH), o_b.reshape(m, B_WIDTH), x2, p2, f_conv, post_w, seq)
    new_states = (wkv_new, shift_new[:, 0, :], ssm_new, conv_new, fconv_new)
    return y2.reshape(bsz, seq, D_MODEL), tuple(s[None] for s in new_states)


def kernel(x_prompt, x_sample, p_prompt, p_sample, state_a_wkv, state_a_shift, state_b_ssm, state_b_conv, state_ffn_conv, w_in, a_mu, a_w0, a_w_w2, a_a0, a_w_a2, a_w_g2, a_k_k, a_k_a, a_r_k, a_gn_g, a_gn_b, b_conv_w, b_a_log, b_dt_bias, b_norm_g, w_o, ln1_g, ln1_b, w_up, f_conv_w, f_conv_b, w_down, ln2_g, ln2_b, w_ple, ple_g, w_ple_gate):
    assert w_in.shape[0] == DEPTH
    weights = _prepare_weights(w_in, a_mu, a_w0, a_w_w2, a_a0, a_w_a2, a_w_g2, a_k_k, a_k_a, a_r_k,
                               a_gn_g, a_gn_b, b_conv_w, b_a_log, b_dt_bias, b_norm_g, w_o, ln1_g,
                               ln1_b, w_up, f_conv_w, f_conv_b, w_down, ln2_g, ln2_b, w_ple, ple_g,
                               w_ple_gate)
    bp = x_prompt.shape[0]
    zeros = lambda *s: jnp.zeros((DEPTH, bp) + s, x_prompt.dtype)
    prompt_init = (zeros(A_HEADS, A_HEAD, A_HEAD), zeros(A_SHIFT_W), zeros(B_HEADS, B_HEAD, B_HEAD),
                   zeros(B_CONV - 1, B_QKV), zeros(FFN_CONV - 1, D_FF))
    y_prompt, prompt_states = _trunk(x_prompt, p_prompt[0], prompt_init, weights)
    sample_init = (state_a_wkv, state_a_shift, state_b_ssm, state_b_conv, state_ffn_conv)
    y_sample, sample_states = _trunk(x_sample, p_sample[0], sample_init, weights)
    return (y_prompt, y_sample) + prompt_states + sample_states
```

```python
import functools

import jax
import jax.numpy as jnp
from jax import lax
from jax.experimental import pallas as pl
from jax.experimental.pallas import tpu as pltpu

F32 = jnp.float32
BF16 = jnp.bfloat16

LANES = 128
SUBLANES = 8
MXU_TILE = 256
VMEM_LIMIT_BYTES = 56 * 1024 * 1024

D_MODEL = 1024
A_WIDTH = 512
A_HEAD = 64
A_HEADS = 8
A_SHIFT_W = 1792
B_WIDTH = 512
B_HEAD = 128
B_HEADS = 4
B_CONV = 4
B_QKV = 1536
B_MIX_W = 2048
D_FF = 2816
FFN_CONV = 3
PLE_DIM = 256
DEPTH = 1
DN_ALPHA = (2.0 * DEPTH) ** 0.25
LN_EPS = 1e-5
GN_EPS = 64e-5
RMS_EPS = 1e-6
L2_EPS = 1e-12

SLOT_ROWS = 64
POST_ROWS = (512, 256)
PROJ_ROWS = 512
FF_CUTS = (0, 1536, 2816)
RWKV_SLOTS = ((8, 4), (2, 2))
GDN_SLOTS = ((8, 8), (2, 2))


def _dot(a, b):
    return jnp.dot(a, b, preferred_element_type=F32)


def _mm(a, b):
    return _dot(a.astype(BF16), b.astype(BF16))


def _split(x, n):
    parts = []
    r = x
    for i in range(n):
        h = r.astype(BF16)
        parts.append(h)
        if i + 1 < n:
            r = r - h.astype(F32)
    return parts


def _mm_exact_lhs(m, x, n):
    out = None
    for part in _split(x, n):
        d = _dot(m, part)
        out = d if out is None else out + d
    return out


def _mm_exact_rhs(x, m, n):
    out = None
    for part in _split(x, n):
        d = _dot(part, m)
        out = d if out is None else out + d
    return out


def _segment_sums(x, log_seg, n_split):
    tile = min(MXU_TILE, x.shape[1])
    same = (jnp.right_shift(_iota((tile, tile), 0), log_seg) == jnp.right_shift(_iota((tile, tile), 1), log_seg))
    ones = jnp.where(same, 1.0, 0.0).astype(BF16)
    parts = [_mm_exact_rhs(x[:, s:s + tile], ones, n_split) for s in range(0, x.shape[1], tile)]
    return parts[0] if len(parts) == 1 else jnp.concatenate(parts, axis=1)


def _sigmoid(x):
    return 1.0 / (1.0 + jnp.exp(-x))


def _silu(x):
    return x * _sigmoid(x)


def _softplus(x):
    return jnp.maximum(x, 0.0) + jnp.log1p(jnp.exp(-jnp.abs(x)))


def _iota(shape, dim):
    return lax.broadcasted_iota(jnp.int32, shape, dim)


def _shift_rows(x, j, prev, c):
    rows = x.shape[0]
    rolled = pltpu.roll(x, j, 0)
    if c == SUBLANES:
        t = jnp.bitwise_and(_iota((rows, 1), 0), SUBLANES - 1)
        return jnp.where(t >= j, rolled, pltpu.roll(prev, rows - SUBLANES + j, 0))
    t = _iota((SUBLANES, 1), 0)
    parts = []
    for i in range(rows // c):
        head = jnp.where(t >= j, rolled[i * c:i * c + SUBLANES],
                         pltpu.roll(prev[i * SUBLANES:(i + 1) * SUBLANES], j, 0))
        parts += [head, rolled[i * c + SUBLANES:(i + 1) * c]]
    return jnp.concatenate(parts, axis=0)


def _load_prev(prev_scr, buf_ref, k_prev):
    prev_scr[...] = jnp.zeros(prev_scr.shape, F32)
    for i in range(buf_ref.shape[0]):
        prev_scr[SUBLANES * (i + 1) - k_prev:SUBLANES * (i + 1), :] = buf_ref[i]


def _store_tails(x, c, k_prev, prev_scr, tail_ref, seq0=0):
    for i in range(x.shape[0] // c):
        q = seq0 + i
        tail_ref[q] = x[(i + 1) * c - k_prev:(i + 1) * c, :]
        if c > SUBLANES:
            prev_scr[q * SUBLANES:(q + 1) * SUBLANES, :] = x[(i + 1) * c - SUBLANES:(i + 1) * c, :]


def _run_waves(n_waves, pre, work):
    def drain(gen):
        while True:
            try:
                next(gen)
            except StopIteration as stop:
                return stop.value

    ready = drain(pre(0))
    for w in range(n_waves):
        cur = work(w, ready)
        nxt = pre(w + 1) if w + 1 < n_waves else None
        cur_done = False
        while not cur_done or nxt is not None:
            if not cur_done:
                try:
                    next(cur)
                except StopIteration:
                    cur_done = True
            if nxt is not None:
                try:
                    next(nxt)
                except StopIteration as stop:
                    ready = stop.value
                    nxt = None


def _seq_cumsum_matrix(rows, log_c):
    r = _iota((rows, rows), 0)
    q = _iota((rows, rows), 1)
    same = jnp.right_shift(r, log_c) == jnp.right_shift(q, log_c)
    return jnp.where(jnp.logical_and(same, r >= q), 1.0, 0.0).astype(BF16)


def _folded_masks(rows, log_c):
    r = _iota((rows, 2 * rows), 0)
    q = jnp.bitwise_and(_iota((rows, 2 * rows), 1), rows - 1)
    same = jnp.right_shift(r, log_c) == jnp.right_shift(q, log_c)
    return jnp.logical_and(same, r > q), jnp.logical_and(same, r >= q)


def _proj_kernel(x_ref, w_ref, cb_ref, cw_ref, pa_ref, pb_ref, pg_ref, cout_ref, prev_scr, *, c):
    t_idx = pl.program_id(1)

    @pl.when(t_idx == 0)
    def _init():
        _load_prev(prev_scr, cb_ref, B_CONV - 1)

    xb = x_ref[...].astype(BF16)
    in_w = w_ref.shape[1]
    gates_w = in_w - A_SHIFT_W - B_QKV - B_WIDTH
    pb = _dot(xb, w_ref[:, A_SHIFT_W:in_w])
    x = pb[:, 0:B_QKV]
    prev = prev_scr[...]
    conv = (cw_ref[0:1, :] * _shift_rows(x, 3, prev, c) + cw_ref[1:2, :] * _shift_rows(x, 2, prev, c)
            + cw_ref[2:3, :] * _shift_rows(x, 1, prev, c) + cw_ref[3:4, :] * x)
    _store_tails(x, c, B_CONV - 1, prev_scr, cout_ref)
    qkv = _silu(conv)
    qk = qkv[:, 0:2 * B_WIDTH]
    qk = qk * lax.rsqrt(_segment_sums(qk * qk, 7, 1) + L2_EPS)
    pb_ref[:, 0:B_WIDTH] = (qk[:, 0:B_WIDTH] * (B_HEAD ** -0.5)).astype(BF16)
    pb_ref[:, B_WIDTH:2 * B_WIDTH] = qk[:, B_WIDTH:2 * B_WIDTH].astype(BF16)
    pb_ref[:, 2 * B_WIDTH:B_QKV] = qkv[:, 2 * B_WIDTH:B_QKV].astype(BF16)
    pb_ref[:, B_QKV:B_QKV + B_WIDTH] = _silu(pb[:, B_QKV:B_QKV + B_WIDTH]).astype(BF16)
    pg_ref[:, 0:gates_w] = pb[:, B_QKV + B_WIDTH:B_QKV + B_WIDTH + gates_w]
    pg_ref[:, gates_w:LANES] = jnp.zeros((pg_ref.shape[0], LANES - gates_w), F32)
    pa_ref[...] = _dot(xb, w_ref[:, 0:A_SHIFT_W]).astype(BF16)


def _const_spec(shape):
    zeros = (0,) * len(shape)
    return pl.BlockSpec(shape, lambda *_: zeros, pipeline_mode=pl.Buffered(1))


def _seq_blocking(rows, seq, n_seq):
    c = rows if seq % rows == 0 else seq
    nb = rows // c
    return c, nb, seq // c, n_seq // nb


def _proj_call(x2, conv0, w_in, conv_w, seq):
    m = x2.shape[0]
    rows = PROJ_ROWS
    c, nb, n_t, groups = _seq_blocking(rows, seq, conv0.shape[0])
    prev_rows = SUBLANES if nb == 1 else rows
    row_map = lambda g, t: (g * n_t + t, 0)
    grp3 = lambda g, t: (g, 0, 0)
    return pl.pallas_call(
        functools.partial(_proj_kernel, c=c),
        grid=(groups, n_t),
        in_specs=[pl.BlockSpec((rows, D_MODEL), row_map),
                  _const_spec(w_in.shape),
                  pl.BlockSpec((nb, B_CONV - 1, B_QKV), grp3),
                  _const_spec(conv_w.shape)],
        out_specs=[pl.BlockSpec((rows, A_SHIFT_W), row_map),
                   pl.BlockSpec((rows, B_MIX_W), row_map),
                   pl.BlockSpec((rows, LANES), row_map),
                   pl.BlockSpec((nb, B_CONV - 1, B_QKV), grp3)],
        out_shape=[jax.ShapeDtypeStruct((m, A_SHIFT_W), BF16),
                   jax.ShapeDtypeStruct((m, B_MIX_W), BF16),
                   jax.ShapeDtypeStruct((m, LANES), F32),
                   jax.ShapeDtypeStruct(conv0.shape, F32)],
        scratch_shapes=[pltpu.VMEM((prev_rows, B_QKV), F32)],
        compiler_params=pltpu.CompilerParams(
            dimension_semantics=("arbitrary", "arbitrary"), vmem_limit_bytes=VMEM_LIMIT_BYTES),
        name="in_proj",
    )(x2, w_in, conv0, conv_w)


def _rwkv_kernel(pa_ref, s0_ref, sh0_ref, mu_ref, vec_ref, ww2_ref, wa2_ref, wg2_ref,
                 o_ref, sfin_ref, shout_ref, s_scr, prev_scr, *, ns, wave, nb, log_c, n_t):
    c = 1 << log_c
    rows = SLOT_ROWS
    n2 = 2 * rows
    pairs = A_HEADS // 2
    n_seq = ns * nb
    t_idx = pl.program_id(1)

    @pl.when(t_idx == 0)
    def _init():
        z = jnp.zeros((A_HEAD, A_HEAD), F32)
        for i in range(n_seq):
            for p in range(pairs):
                top = jnp.concatenate([s0_ref[i, 2 * p], z], axis=1)
                bot = jnp.concatenate([z, s0_ref[i, 2 * p + 1]], axis=1)
                s_scr[i, p] = jnp.concatenate([top, bot], axis=0).T
        _load_prev(prev_scr, sh0_ref, 1)

    w0 = vec_ref[0:1, :]
    a0 = vec_ref[1:2, :]
    k_k = vec_ref[2:3, :]
    k_a = vec_ref[3:4, :]
    r_k = vec_ref[4:5, :]
    gn_g = vec_ref[5:6, :]
    gn_b = vec_ref[6:7, :]

    def head_sum(x, n_split):
        return _segment_sums(x, 6, n_split)

    strict, incl = _folded_masks(rows, log_c)
    lane0 = _iota((1, LANES), 1) < A_HEAD
    col_seq = jnp.bitwise_and(jnp.right_shift(_iota((1, n2), 1), log_c), nb - 1)

    def stack(xp):
        return jnp.concatenate([jnp.where(lane0, xp, 0.0), jnp.where(lane0, 0.0, xp)], axis=0)

    def seq_cols(xt, b):
        return xt if nb == 1 else jnp.where(col_seq == b, xt, 0.0)

    def pre(w):
        slot0 = w * wave
        seq0 = slot0 * nb
        u = pa_ref[slot0:slot0 + wave].reshape(wave * rows, A_SHIFT_W).astype(F32)
        prev = prev_scr[seq0 * SUBLANES:(seq0 + wave * nb) * SUBLANES, :]
        xs = u + (_shift_rows(u, 1, prev, c) - u) * mu_ref[...]
        _store_tails(u, c, 1, prev_scr, shout_ref, seq0)
        r = xs[:, 0:512]
        k = xs[:, 512:1024]
        v = xs[:, 1024:1536]
        wa = xs[:, 1536:1664]
        gd = xs[:, 1664:1792]
        wlora = _dot(jnp.tanh(wa).astype(BF16), ww2_ref[...])
        alora = _dot(wa.astype(BF16), wa2_ref[...])
        g = _dot(_sigmoid(gd).astype(BF16), wg2_ref[...])
        kkr = k * k_k
        kk_ss = head_sum(kkr * kkr, 1)
        yield
        wlog = -_softplus(-(w0 + wlora)) - 0.5
        lw = -jnp.exp(wlog)
        gcum = _mm_exact_lhs(_seq_cumsum_matrix(wave * rows, log_c), lw, 2)
        a = _sigmoid(a0 + alora)
        kk = kkr * lax.rsqrt(kk_ss + L2_EPS)
        k2 = k * (1.0 + (a - 1.0) * k_a)
        bonus = head_sum(r * k2 * r_k, 1) * v
        yield
        e_g = jnp.exp(gcum)
        e_gi = jnp.exp(-gcum)
        a_t = -kk * jnp.exp(gcum - lw)
        b_t = kk * a * e_gi
        k_t = k2 * e_gi
        r_t = r * e_g
        return a_t, b_t, k_t, r_t, v, e_g, bonus, g

    def work(w, prepared):
        a_t, b_t, k_t, r_t, v, e_g, bonus, g = prepared
        slot0 = w * wave
        chains = [(s, p) for s in range(wave) for p in range(pairs)]
        a_l, r_l, v_l, bt_l, kt_l, n_l, ak_l, rbk_l, dec_l = ([] for _ in range(9))
        for s, p in chains:
            rs = slice(s * rows, (s + 1) * rows)
            sl = slice(p * LANES, (p + 1) * LANES)
            bk_t = jnp.concatenate([stack(b_t[rs, sl]), stack(k_t[rs, sl])], axis=0).T
            x = _mm(jnp.concatenate([a_t[rs, sl], r_t[rs, sl]], axis=0), bk_t)
            a_l.append(a_t[rs, sl])
            r_l.append(r_t[rs, sl])
            v_l.append(stack(v[rs, sl]).astype(BF16))
            bt_l.append(bk_t[:, 0:n2])
            kt_l.append(bk_t[:, n2:2 * n2])
            n_l.append(jnp.where(strict, x[0:rows, 0:n2], 0.0))
            ak_l.append(jnp.where(strict, x[0:rows, n2:2 * n2], 0.0))
            rbk_l.append(jnp.concatenate([jnp.where(incl, x[rows:n2, 0:n2], 0.0),
                                          jnp.where(incl, x[rows:n2, n2:2 * n2], 0.0)], axis=1))
            dec_l.append(e_g[rs, sl].T)
        yield
        tm_l = list(n_l)
        pw_l = list(n_l)
        pwb_l = [stack(q).astype(BF16) for q in pw_l]
        for _ in range(log_c - 1):
            pw_l = [_mm(q, qb) for q, qb in zip(pw_l, pwb_l)]
            pwb_l = [stack(q).astype(BF16) for q in pw_l]
            tm_l = [t + q + _mm(t, qb) for t, q, qb in zip(tm_l, pw_l, pwb_l)]
            yield
        akv_l = [_mm(ak, v_s) for ak, v_s in zip(ak_l, v_l)]
        kv_l = [[_mm(seq_cols(kt_l[ci], b), v_l[ci]) for b in range(nb)] for ci in range(len(chains))]
        yield
        wu_l = []
        for ci in range(len(chains)):
            rhs = jnp.concatenate([a_l[ci], akv_l[ci]], axis=1)
            rhs_s = jnp.concatenate([stack(a_l[ci]), stack(akv_l[ci])], axis=1)
            wu_l.append(rhs + _mm(tm_l[ci], rhs_s))
        yield
        sa_l, rs_l = [], []
        for ci, (s, p) in enumerate(chains):
            sa_parts, rs_parts = [], []
            for b in range(nb):
                m_old = s_scr[(slot0 + s) * nb + b, p]
                sa_parts.append(_mm(wu_l[ci][b * c:(b + 1) * c, 0:LANES], m_old)
                                + wu_l[ci][b * c:(b + 1) * c, LANES:2 * LANES])
                rs_parts.append(_mm(r_l[ci][b * c:(b + 1) * c], m_old))
            sa_l.append(sa_parts[0] if nb == 1 else jnp.concatenate(sa_parts, axis=0))
            rs_l.append(rs_parts[0] if nb == 1 else jnp.concatenate(rs_parts, axis=0))
        yield
        sas_l = [stack(sa).astype(BF16) for sa in sa_l]
        for ci, (s, p) in enumerate(chains):
            for b in range(nb):
                q = (slot0 + s) * nb + b
                upd = _mm(seq_cols(bt_l[ci], b), sas_l[ci]) + kv_l[ci][b]
                s_scr[q, p] = (s_scr[q, p] + upd) * dec_l[ci][:, (b + 1) * c - 1:(b + 1) * c]
        o_rows = []
        for s in range(wave):
            o_parts = []
            for p in range(pairs):
                ci = s * pairs + p
                o_parts.append(rs_l[ci] + _mm(rbk_l[ci], jnp.concatenate([sas_l[ci], v_l[ci]], axis=0)))
            o_rows.append(jnp.concatenate(o_parts, axis=1))
        o = jnp.concatenate(o_rows, axis=0)
        yield
        mean = head_sum(o, 2) * (1.0 / A_HEAD)
        d = o - mean
        var = head_sum(d * d, 1) * (1.0 / A_HEAD)
        out = (d * lax.rsqrt(var + GN_EPS) * gn_g + gn_b + bonus) * g
        o_ref[slot0:slot0 + wave] = out.reshape(wave, rows, A_WIDTH)

    _run_waves(ns // wave, pre, work)

    @pl.when(t_idx == n_t - 1)
    def _fin():
        for i in range(n_seq):
            for p in range(pairs):
                s_pair = s_scr[i, p].T
                sfin_ref[i, 2 * p] = s_pair[0:A_HEAD, 0:A_HEAD]
                sfin_ref[i, 2 * p + 1] = s_pair[A_HEAD:LANES, A_HEAD:LANES]


def _mixer_specs(ns, nb, widths_in, width_out, state_block):
    n_seq = ns * nb
    data = lambda g, t: (g, t, 0)
    grp4 = lambda g, t: (g, 0, 0, 0)
    in_specs = ([pl.BlockSpec((ns, SLOT_ROWS, width), data) for width in widths_in]
                + [pl.BlockSpec((n_seq,) + state_block, grp4)])
    out_specs = [pl.BlockSpec((ns, SLOT_ROWS, width_out), data),
                 pl.BlockSpec((n_seq,) + state_block, grp4)]
    return in_specs, out_specs


def _rwkv_call(proj_a, shift0, wkv0, wts, ns, wave, nb, log_c):
    n_t = proj_a.shape[1] // SLOT_ROWS
    prev_rows = (SUBLANES if nb == 1 else SLOT_ROWS) * ns
    in_specs, out_specs = _mixer_specs(ns, nb, (A_SHIFT_W,), A_WIDTH, (A_HEADS, A_HEAD, A_HEAD))
    shift_spec = pl.BlockSpec((ns * nb, 1, A_SHIFT_W), lambda g, t: (g, 0, 0))
    kern = functools.partial(_rwkv_kernel, ns=ns, wave=wave, nb=nb, log_c=log_c, n_t=n_t)
    return pl.pallas_call(
        kern,
        grid=(proj_a.shape[0] // ns, n_t),
        in_specs=in_specs + [shift_spec] + [_const_spec(w.shape) for w in wts],
        out_specs=out_specs + [shift_spec],
        out_shape=[jax.ShapeDtypeStruct(proj_a.shape[:2] + (A_WIDTH,), F32),
                   jax.ShapeDtypeStruct(wkv0.shape, F32),
                   jax.ShapeDtypeStruct(shift0.shape, F32)],
        scratch_shapes=[pltpu.VMEM((ns * nb, A_HEADS // 2, LANES, LANES), F32),
                        pltpu.VMEM((prev_rows, A_SHIFT_W), F32)],
        compiler_params=pltpu.CompilerParams(
            dimension_semantics=("arbitrary", "arbitrary"), vmem_limit_bytes=VMEM_LIMIT_BYTES),
        name="rwkv7_mix",
    )(proj_a, wkv0, shift0, *wts)


def _gdn_kernel(pb_ref, pg_ref, s0_ref, hv_ref, o_ref, sfin_ref, s_scr, *, ns, wave, nb, log_c, n_t):
    c = 1 << log_c
    rows = SLOT_ROWS
    n2 = 2 * rows
    units = 2 * nb
    npairs = B_HEADS // 2
    t_idx = pl.program_id(1)

    @pl.when(t_idx == 0)
    def _init():
        s_scr[...] = s0_ref[...]

    a_log = hv_ref[0:1, :]
    dt_bias = hv_ref[1:2, :]
    norm_g = jnp.concatenate([hv_ref[2:3, :]] * B_HEADS, axis=1)

    strict, incl = _folded_masks(rows, log_c)
    lane0 = _iota((1, LANES), 1) < rows

    def stack(xf):
        return jnp.concatenate([jnp.where(lane0, xf, 0.0), jnp.where(lane0, 0.0, xf)], axis=0)

    row2 = _iota((n2, n2), 0)
    col2 = _iota((n2, n2), 1)
    last_sel_t = row2 == jnp.bitwise_or(col2, c - 1)
    col_unit = jnp.right_shift(_iota((1, n2), 1), log_c)

    def pre(w):
        slot0 = w * wave
        pb = pb_ref[slot0:slot0 + wave].reshape(wave * rows, B_MIX_W)
        ba = pg_ref[slot0:slot0 + wave].reshape(wave * rows, LANES)
        beta_t = _sigmoid(ba)
        g_t = -jnp.exp(a_log) * _softplus(ba + dt_bias)
        gcum_t = _mm_exact_lhs(_seq_cumsum_matrix(wave * rows, log_c), g_t, 3)
        yield
        return (pb[:, 0:B_WIDTH], pb[:, B_WIDTH:2 * B_WIDTH], pb[:, 2 * B_WIDTH:B_QKV], beta_t, gcum_t,
                pb[:, B_QKV:B_MIX_W].astype(F32))

    def state_index(slot, p, unit):
        return slot * nb + unit % nb, 2 * p + unit // nb

    def work(w, prepared):
        qn, kn, vv, beta_t, gcum_t, zs = prepared
        slot0 = w * wave
        chains = [(s, p) for s in range(wave) for p in range(npairs)]
        nl_l, rhs_l, attn_l, qdec_l, kdt_l, gl_l = ([] for _ in range(6))
        for s, p in chains:
            rs = slice(s * rows, (s + 1) * rows)
            hs = (2 * p, 2 * p + 1)
            q_s = jnp.concatenate([qn[rs, h * LANES:(h + 1) * LANES] for h in hs], axis=0)
            k_s = jnp.concatenate([kn[rs, h * LANES:(h + 1) * LANES] for h in hs], axis=0)
            v_s = jnp.concatenate([vv[rs, h * LANES:(h + 1) * LANES] for h in hs], axis=0)
            beta_s = jnp.broadcast_to(jnp.concatenate([beta_t[rs, h:h + 1] for h in hs], axis=0), (n2, LANES))
            g_s = jnp.concatenate([gcum_t[rs, B_HEADS + h:B_HEADS + h + 1] for h in hs], axis=0)
            gi = jnp.broadcast_to(g_s, (n2, n2))
            gj = gi.T
            gi_f = jnp.where(lane0, gi[0:rows], gi[rows:n2])
            dec_f = jnp.where(incl, jnp.exp(gi_f - gj[0:rows]), 0.0)
            kb = k_s * beta_s
            ks_t = k_s.T
            x = _mm(jnp.concatenate([kb, q_s], axis=0), ks_t)
            e_gs = jnp.exp(gi)
            g_last = jnp.sum(jnp.where(last_sel_t, gi, 0.0), axis=0, keepdims=True)
            l_f = jnp.where(lane0, x[0:rows], x[rows:n2])
            nl_l.append(-(l_f * jnp.where(strict, dec_f, 0.0)))
            attn_l.append((x[n2:2 * n2] * stack(dec_f)).astype(BF16))
            rhs_l.append(jnp.concatenate([v_s * beta_s, kb * e_gs], axis=1))
            qdec_l.append(q_s * e_gs)
            kdt_l.append(ks_t * jnp.exp(g_last - gj[0:1, :]))
            gl_l.append(jnp.exp(g_last))
        yield
        tm_l = list(nl_l)
        pw_l = list(nl_l)
        pwb_l = [stack(q).astype(BF16) for q in pw_l]
        for _ in range(log_c - 1):
            pw_l = [_mm(q, qb) for q, qb in zip(pw_l, pwb_l)]
            pwb_l = [stack(q).astype(BF16) for q in pw_l]
            tm_l = [t + q + _mm(t, qb) for t, q, qb in zip(tm_l, pw_l, pwb_l)]
            yield
        uw_l = [rhs + _mm(stack(tm), rhs) for tm, rhs in zip(tm_l, rhs_l)]
        yield
        vn_l, qs_l = [], []
        for ci, (s, p) in enumerate(chains):
            vn_parts, qs_parts = [], []
            for unit in range(units):
                s_old = s_scr[state_index(slot0 + s, p, unit)]
                vn_parts.append(uw_l[ci][unit * c:(unit + 1) * c, 0:LANES]
                                - _mm(uw_l[ci][unit * c:(unit + 1) * c, LANES:2 * LANES], s_old))
                qs_parts.append(_mm(qdec_l[ci][unit * c:(unit + 1) * c], s_old))
            vn_l.append(jnp.concatenate(vn_parts, axis=0).astype(BF16))
            qs_l.append(jnp.concatenate(qs_parts, axis=0))
        yield
        for ci, (s, p) in enumerate(chains):
            for unit in range(units):
                kd_u = jnp.where(col_unit == unit, kdt_l[ci], 0.0)
                idx = state_index(slot0 + s, p, unit)
                s_scr[idx] = s_scr[idx] * gl_l[ci][:, unit * c:unit * c + 1] + _mm(kd_u, vn_l[ci])
        o_rows = []
        for s in range(wave):
            heads = []
            for p in range(npairs):
                ci = s * npairs + p
                o_s = qs_l[ci] + _mm(attn_l[ci], vn_l[ci])
                heads += [o_s[0:rows], o_s[rows:n2]]
            o_rows.append(jnp.concatenate(heads, axis=1))
        o = jnp.concatenate(o_rows, axis=0)
        yield
        mean_sq = _segment_sums(o * o, 7, 1) * (1.0 / B_HEAD)
        out = o * lax.rsqrt(mean_sq + RMS_EPS) * norm_g * zs
        o_ref[slot0:slot0 + wave] = out.reshape(wave, rows, B_WIDTH)

    _run_waves(ns // wave, pre, work)

    @pl.when(t_idx == n_t - 1)
    def _fin():
        sfin_ref[...] = s_scr[...]


def _gdn_call(proj_b, proj_g, ssm0, wts, ns, wave, nb, log_c):
    n_t = proj_b.shape[1] // SLOT_ROWS
    in_specs, out_specs = _mixer_specs(ns, nb, (B_MIX_W, LANES), B_WIDTH, (B_HEADS, B_HEAD, B_HEAD))
    kern = functools.partial(_gdn_kernel, ns=ns, wave=wave, nb=nb, log_c=log_c, n_t=n_t)
    return pl.pallas_call(
        kern,
        grid=(proj_b.shape[0] // ns, n_t),
        in_specs=in_specs + [_const_spec(w.shape) for w in wts],
        out_specs=out_specs,
        out_shape=[jax.ShapeDtypeStruct(proj_b.shape[:2] + (B_WIDTH,), F32),
                   jax.ShapeDtypeStruct(ssm0.shape, F32)],
        scratch_shapes=[pltpu.VMEM((ns * nb, B_HEADS, B_HEAD, B_HEAD), F32)],
        compiler_params=pltpu.CompilerParams(
            dimension_semantics=("arbitrary", "arbitrary"), vmem_limit_bytes=VMEM_LIMIT_BYTES),
        name="gdn_mix",
    )(proj_b, proj_g, ssm0, *wts)


def _layer_norm(x, g, b):
    mu = jnp.mean(x, axis=-1, keepdims=True)
    d = x - mu
    var = jnp.mean(d * d, axis=-1, keepdims=True)
    return d * lax.rsqrt(var + LN_EPS) * g + b


def _post_kernel(oa_ref, ob_ref, x_ref, p_ref, fb_ref, woa_ref, wob_ref, lnv_ref, wgu_ref,
                 fv_ref, wd_ref, wple_ref, wpg_ref, y_ref, fout_ref, prev_scr, *, c):
    t_idx = pl.program_id(1)

    @pl.when(t_idx == 0)
    def _init():
        _load_prev(prev_scr, fb_ref, FFN_CONV - 1)

    ln1_g, ln1_b = lnv_ref[0:1, :], lnv_ref[1:2, :]
    ln2_g, ln2_b = lnv_ref[2:3, :], lnv_ref[3:4, :]
    ple_g = lnv_ref[4:5, :]

    mix = _dot(oa_ref[...].astype(BF16), woa_ref[...]) + _dot(ob_ref[...].astype(BF16), wob_ref[...])
    h = _layer_norm(DN_ALPHA * x_ref[...] + mix, ln1_g, ln1_b)
    hb = h.astype(BF16)
    ffn = None
    for f0, f1 in zip(FF_CUTS[:-1], FF_CUTS[1:]):
        fs = slice(f0, f1)
        gate = _dot(hb, wgu_ref[:, fs])
        up = _dot(hb, wgu_ref[:, D_FF + f0:D_FF + f1])
        prev = prev_scr[:, fs]
        gate_c = (fv_ref[0:1, fs] * _shift_rows(gate, 2, prev, c) + fv_ref[1:2, fs] * _shift_rows(gate, 1, prev, c)
                  + fv_ref[2:3, fs] * gate)
        _store_tails(gate, c, FFN_CONV - 1, prev_scr.at[:, fs], fout_ref.at[:, :, fs])
        act = _silu(gate_c + fv_ref[3:4, fs]) * up
        part = _dot(act.astype(BF16), wd_ref[fs, :])
        ffn = part if ffn is None else ffn + part
    h2 = _layer_norm(DN_ALPHA * h + ffn, ln2_g, ln2_b)
    pe = _dot(p_ref[...].astype(BF16), wple_ref[...])
    e = pe * lax.rsqrt(jnp.mean(pe * pe, axis=-1, keepdims=True) + RMS_EPS) * ple_g
    y_ref[...] = h2 + _sigmoid(_dot(h2.astype(BF16), wpg_ref[...])) * e


def _post_call(o_a, o_b, x2, p2, fconv0, wts, seq):
    rows = POST_ROWS[0] if seq % POST_ROWS[0] == 0 else POST_ROWS[1]
    c, nb, n_t, groups = _seq_blocking(rows, seq, fconv0.shape[0])
    prev_rows = SUBLANES if nb == 1 else rows
    row_map = lambda g, t: (g * n_t + t, 0)
    grp3 = lambda g, t: (g, 0, 0)
    kern = functools.partial(_post_kernel, c=c)
    return pl.pallas_call(
        kern,
        grid=(groups, n_t),
        in_specs=[pl.BlockSpec((rows, A_WIDTH), row_map),
                  pl.BlockSpec((rows, B_WIDTH), row_map),
                  pl.BlockSpec((rows, D_MODEL), row_map),
                  pl.BlockSpec((rows, PLE_DIM), row_map),
                  pl.BlockSpec((nb, FFN_CONV - 1, D_FF), grp3)]
                 + [_const_spec(w.shape) for w in wts],
        out_specs=[pl.BlockSpec((rows, D_MODEL), row_map),
                   pl.BlockSpec((nb, FFN_CONV - 1, D_FF), grp3)],
        out_shape=[jax.ShapeDtypeStruct(x2.shape, F32),
                   jax.ShapeDtypeStruct(fconv0.shape, F32)],
        scratch_shapes=[pltpu.VMEM((prev_rows, D_FF), F32)],
        compiler_params=pltpu.CompilerParams(
            dimension_semantics=("arbitrary", "arbitrary"), vmem_limit_bytes=VMEM_LIMIT_BYTES),
        name="post_mix_ffn",
    )(o_a, o_b, x2, p2, fconv0, *wts)


def _pad_rows(w, n):
    return jnp.pad(w, ((0, n - w.shape[0]), (0, 0)))


def _prepare_weights(w_in, a_mu, a_w0, a_w_w2, a_a0, a_w_a2, a_w_g2, a_k_k, a_k_a, a_r_k, a_gn_g,
                     a_gn_b, b_conv_w, b_a_log, b_dt_bias, b_norm_g, w_o, ln1_g, ln1_b, w_up,
                     f_conv_w, f_conv_b, w_down, ln2_g, ln2_b, w_ple, ple_g, w_ple_gate):
    lora = 64
    proj = (w_in[0].astype(BF16), _pad_rows(b_conv_w[0], SUBLANES))
    rwkv = (a_mu,
            _pad_rows(jnp.concatenate([a_w0, a_a0, a_k_k, a_k_a, a_r_k, a_gn_g, a_gn_b], axis=0), SUBLANES),
            _pad_rows(a_w_w2[0], LANES).astype(BF16),
            jnp.pad(a_w_a2[0], ((lora, LANES - 2 * lora), (0, 0))).astype(BF16),
            a_w_g2[0].astype(BF16))
    head_row = lambda vec: jnp.pad(vec, ((0, 0), (B_HEADS, LANES - 2 * B_HEADS)))
    gdn = (_pad_rows(jnp.concatenate([head_row(b_a_log), head_row(b_dt_bias), b_norm_g], axis=0), SUBLANES),)
    post = (w_o[0, :A_WIDTH].astype(BF16), w_o[0, A_WIDTH:].astype(BF16),
            _pad_rows(jnp.concatenate([ln1_g, ln1_b, ln2_g, ln2_b, ple_g], axis=0), SUBLANES),
            w_up[0].astype(BF16),
            _pad_rows(jnp.concatenate([f_conv_w[0], f_conv_b], axis=0), SUBLANES),
            w_down[0].astype(BF16), w_ple[0].astype(BF16), w_ple_gate[0].astype(BF16))
    return proj, rwkv, gdn, post


def _trunk(x, p, states, weights):
    proj_w, rwkv_w, gdn_w, post_w = weights
    a_wkv, a_shift, b_ssm, b_conv, f_conv = (s[0] for s in states)
    bsz, seq, _ = x.shape
    m = bsz * seq
    x2 = x.reshape(m, D_MODEL)
    p2 = p.reshape(m, PLE_DIM)
    if seq % SLOT_ROWS == 0:
        nb, log_c, slots, kind = 1, 6, bsz, 0
    else:
        assert seq == SUBLANES
        nb, log_c, slots, kind = SLOT_ROWS // seq, 3, m // SLOT_ROWS, 1
    proj_a, proj_b, proj_g, conv_new = _proj_call(x2, b_conv, *proj_w, seq)
    o_a, wkv_new, shift_new = _rwkv_call(proj_a.reshape(slots, -1, A_SHIFT_W), a_shift[:, None, :], a_wkv,
                                         rwkv_w, *RWKV_SLOTS[kind], nb, log_c)
    o_b, ssm_new = _gdn_call(proj_b.reshape(slots, -1, B_MIX_W), proj_g.reshape(slots, -1, LANES), b_ssm,
                             gdn_w, *GDN_SLOTS[kind], nb, log_c)
    y2, fconv_new = _post_call(o_a.reshape(m, A_WIDTH), o_b.reshape(m, B_WIDTH), x2, p2, f_conv, post_w, seq)
    new_states = (wkv_new, shift_new[:, 0, :], ssm_new, conv_new, fconv_new)
    return y2.reshape(bsz, seq, D_MODEL), tuple(s[None] for s in new_states)


def kernel(x_prompt, x_sample, p_prompt, p_sample, state_a_wkv, state_a_shift, state_b_ssm, state_b_conv, state_ffn_conv, w_in, a_mu, a_w0, a_w_w2, a_a0, a_w_a2, a_w_g2, a_k_k, a_k_a, a_r_k, a_gn_g, a_gn_b, b_conv_w, b_a_log, b_dt_bias, b_norm_g, w_o, ln1_g, ln1_b, w_up, f_conv_w, f_conv_b, w_down, ln2_g, ln2_b, w_ple, ple_g, w_ple_gate):
    assert w_in.shape[0] == DEPTH
    weights = _prepare_weights(w_in, a_mu, a_w0, a_w_w2, a_a0, a_w_a2, a_w_g2, a_k_k, a_k_a, a_r_k,
                               a_gn_g, a_gn_b, b_conv_w, b_a_log, b_dt_bias, b_norm_g, w_o, ln1_g,
                               ln1_b, w_up, f_conv_w, f_conv_b, w_down, ln2_g, ln2_b, w_ple, ple_g,
                               w_ple_gate)
    bp = x_prompt.shape[0]
    zeros = lambda *s: jnp.zeros((DEPTH, bp) + s, x_prompt.dtype)
    prompt_init = (zeros(A_HEADS, A_HEAD, A_HEAD), zeros(A_SHIFT_W), zeros(B_HEADS, B_HEAD, B_HEAD),
                   zeros(B_CONV - 1, B_QKV), zeros(FFN_CONV - 1, D_FF))
    y_prompt, prompt_states = _trunk(x_prompt, p_prompt[0], prompt_init, weights)
    sample_init = (state_a_wkv, state_a_shift, state_b_ssm, state_b_conv, state_ffn_conv)
    y_sample, sample_states = _trunk(x_sample, p_sample[0], sample_init, weights)
    return (y_prompt, y_sample) + prompt_states + sample_states
```

```python
import functools

import jax
import jax.numpy as jnp
from jax import lax
from jax.experimental import pallas as pl
from jax.experimental.pallas import tpu as pltpu

F32 = jnp.float32
BF16 = jnp.bfloat16

LANES = 128
SUBLANES = 8
MXU_TILE = 256
VMEM_LIMIT_BYTES = 56 * 1024 * 1024

D_MODEL = 1024
A_WIDTH = 512
A_HEAD = 64
A_HEADS = 8
A_SHIFT_W = 1792
B_WIDTH = 512
B_HEAD = 128
B_HEADS = 4
B_CONV = 4
B_QKV = 1536
B_MIX_W = 2048
D_FF = 2816
FFN_CONV = 3
PLE_DIM = 256
DEPTH = 1
DN_ALPHA = (2.0 * DEPTH) ** 0.25
LN_EPS = 1e-5
GN_EPS = 64e-5
RMS_EPS = 1e-6
L2_EPS = 1e-12

SLOT_ROWS = 64
POST_ROWS = (512, 256)
PROJ_ROWS = 512
FF_CUTS = (0, 1536, 2816)
RWKV_SLOTS = ((8, 4), (2, 2))
GDN_SLOTS = ((8, 8), (2, 2))


def _dot(a, b):
    return jnp.dot(a, b, preferred_element_type=F32)


def _mm(a, b):
    return _dot(a.astype(BF16), b.astype(BF16))


def _split(x, n):
    parts = []
    r = x
    for i in range(n):
        h = r.astype(BF16)
        parts.append(h)
        if i + 1 < n:
            r = r - h.astype(F32)
    return parts


def _mm_nt(a, b):
    return lax.dot_general(a.astype(BF16), b.astype(BF16), (((1,), (1,)), ((), ())), preferred_element_type=F32)


def _mm_exact_lhs(m, x, n):
    out = None
    for part in _split(x, n):
        d = _dot(m, part)
        out = d if out is None else out + d
    return out


def _mm_exact_rhs(x, m, n):
    out = None
    for part in _split(x, n):
        d = _dot(part, m)
        out = d if out is None else out + d
    return out


def _segment_sums(x, log_seg, n_split):
    tile = min(MXU_TILE, x.shape[1])
    same = (jnp.right_shift(_iota((tile, tile), 0), log_seg) == jnp.right_shift(_iota((tile, tile), 1), log_seg))
    ones = jnp.where(same, 1.0, 0.0).astype(BF16)
    parts = [_mm_exact_rhs(x[:, s:s + tile], ones, n_split) for s in range(0, x.shape[1], tile)]
    return parts[0] if len(parts) == 1 else jnp.concatenate(parts, axis=1)


def _sigmoid(x):
    return 1.0 / (1.0 + jnp.exp(-x))


def _silu(x):
    return x * _sigmoid(x)


def _softplus(x):
    return jnp.maximum(x, 0.0) + jnp.log1p(jnp.exp(-jnp.abs(x)))


def _iota(shape, dim):
    return lax.broadcasted_iota(jnp.int32, shape, dim)


def _shift_rows(x, j, prev, c):
    rows = x.shape[0]
    rolled = pltpu.roll(x, j, 0)
    if c == SUBLANES:
        t = jnp.bitwise_and(_iota((rows, 1), 0), SUBLANES - 1)
        return jnp.where(t >= j, rolled, pltpu.roll(prev, rows - SUBLANES + j, 0))
    t = _iota((SUBLANES, 1), 0)
    parts = []
    for i in range(rows // c):
        head = jnp.where(t >= j, rolled[i * c:i * c + SUBLANES],
                         pltpu.roll(prev[i * SUBLANES:(i + 1) * SUBLANES], j, 0))
        parts += [head, rolled[i * c + SUBLANES:(i + 1) * c]]
    return jnp.concatenate(parts, axis=0)


def _load_prev(prev_scr, buf_ref, k_prev):
    prev_scr[...] = jnp.zeros(prev_scr.shape, F32)
    for i in range(buf_ref.shape[0]):
        prev_scr[SUBLANES * (i + 1) - k_prev:SUBLANES * (i + 1), :] = buf_ref[i]


def _store_tails(x, c, k_prev, prev_scr, tail_ref, seq0=0):
    for i in range(x.shape[0] // c):
        q = seq0 + i
        tail_ref[q] = x[(i + 1) * c - k_prev:(i + 1) * c, :]
        if c > SUBLANES:
            prev_scr[q * SUBLANES:(q + 1) * SUBLANES, :] = x[(i + 1) * c - SUBLANES:(i + 1) * c, :]


def _run_waves(n_waves, pre, work):
    def drain(gen):
        while True:
            try:
                next(gen)
            except StopIteration as stop:
                return stop.value

    ready = drain(pre(0))
    for w in range(n_waves):
        cur = work(w, ready)
        nxt = pre(w + 1) if w + 1 < n_waves else None
        cur_done = False
        while not cur_done or nxt is not None:
            if not cur_done:
                try:
                    next(cur)
                except StopIteration:
                    cur_done = True
            if nxt is not None:
                try:
                    next(nxt)
                except StopIteration as stop:
                    ready = stop.value
                    nxt = None


def _seq_cumsum_matrix(rows, log_c):
    r = _iota((rows, rows), 0)
    q = _iota((rows, rows), 1)
    same = jnp.right_shift(r, log_c) == jnp.right_shift(q, log_c)
    return jnp.where(jnp.logical_and(same, r >= q), 1.0, 0.0).astype(BF16)


def _folded_masks(rows, log_c):
    r = _iota((rows, 2 * rows), 0)
    q = jnp.bitwise_and(_iota((rows, 2 * rows), 1), rows - 1)
    same = jnp.right_shift(r, log_c) == jnp.right_shift(q, log_c)
    return jnp.logical_and(same, r > q), jnp.logical_and(same, r >= q)


def _proj_kernel(x_ref, w_ref, cb_ref, cw_ref, pa_ref, pb_ref, pg_ref, cout_ref, prev_scr, *, c):
    t_idx = pl.program_id(1)

    @pl.when(t_idx == 0)
    def _init():
        _load_prev(prev_scr, cb_ref, B_CONV - 1)

    xb = x_ref[...].astype(BF16)
    in_w = w_ref.shape[1]
    gates_w = in_w - A_SHIFT_W - B_QKV - B_WIDTH
    pb = _dot(xb, w_ref[:, A_SHIFT_W:in_w])
    x = pb[:, 0:B_QKV]
    prev = prev_scr[...]
    conv = (cw_ref[0:1, :] * _shift_rows(x, 3, prev, c) + cw_ref[1:2, :] * _shift_rows(x, 2, prev, c)
            + cw_ref[2:3, :] * _shift_rows(x, 1, prev, c) + cw_ref[3:4, :] * x)
    _store_tails(x, c, B_CONV - 1, prev_scr, cout_ref)
    qkv = _silu(conv)
    qk = qkv[:, 0:2 * B_WIDTH]
    qk = qk * lax.rsqrt(_segment_sums(qk * qk, 7, 1) + L2_EPS)
    pb_ref[:, 0:B_WIDTH] = (qk[:, 0:B_WIDTH] * (B_HEAD ** -0.5)).astype(BF16)
    pb_ref[:, B_WIDTH:2 * B_WIDTH] = qk[:, B_WIDTH:2 * B_WIDTH].astype(BF16)
    pb_ref[:, 2 * B_WIDTH:B_QKV] = qkv[:, 2 * B_WIDTH:B_QKV].astype(BF16)
    pb_ref[:, B_QKV:B_QKV + B_WIDTH] = _silu(pb[:, B_QKV:B_QKV + B_WIDTH]).astype(BF16)
    pg_ref[:, 0:gates_w] = pb[:, B_QKV + B_WIDTH:B_QKV + B_WIDTH + gates_w]
    pg_ref[:, gates_w:LANES] = jnp.zeros((pg_ref.shape[0], LANES - gates_w), F32)
    pa_ref[...] = _dot(xb, w_ref[:, 0:A_SHIFT_W]).astype(BF16)


def _const_spec(shape):
    zeros = (0,) * len(shape)
    return pl.BlockSpec(shape, lambda *_: zeros, pipeline_mode=pl.Buffered(1))


def _seq_blocking(rows, seq, n_seq):
    c = rows if seq % rows == 0 else seq
    nb = rows // c
    return c, nb, seq // c, n_seq // nb


def _proj_call(x2, conv0, w_in, conv_w, seq):
    m = x2.shape[0]
    rows = PROJ_ROWS
    c, nb, n_t, groups = _seq_blocking(rows, seq, conv0.shape[0])
    prev_rows = SUBLANES if nb == 1 else rows
    row_map = lambda g, t: (g * n_t + t, 0)
    grp3 = lambda g, t: (g, 0, 0)
    return pl.pallas_call(
        functools.partial(_proj_kernel, c=c),
        grid=(groups, n_t),
        in_specs=[pl.BlockSpec((rows, D_MODEL), row_map),
                  _const_spec(w_in.shape),
                  pl.BlockSpec((nb, B_CONV - 1, B_QKV), grp3),
                  _const_spec(conv_w.shape)],
        out_specs=[pl.BlockSpec((rows, A_SHIFT_W), row_map),
                   pl.BlockSpec((rows, B_MIX_W), row_map),
                   pl.BlockSpec((rows, LANES), row_map),
                   pl.BlockSpec((nb, B_CONV - 1, B_QKV), grp3)],
        out_shape=[jax.ShapeDtypeStruct((m, A_SHIFT_W), BF16),
                   jax.ShapeDtypeStruct((m, B_MIX_W), BF16),
                   jax.ShapeDtypeStruct((m, LANES), F32),
                   jax.ShapeDtypeStruct(conv0.shape, F32)],
        scratch_shapes=[pltpu.VMEM((prev_rows, B_QKV), F32)],
        compiler_params=pltpu.CompilerParams(
            dimension_semantics=("arbitrary", "arbitrary"), vmem_limit_bytes=VMEM_LIMIT_BYTES),
        name="in_proj",
    )(x2, w_in, conv0, conv_w)


def _rwkv_kernel(pa_ref, s0_ref, sh0_ref, mu_ref, vec_ref, ww2_ref, wa2_ref, wg2_ref,
                 o_ref, sfin_ref, shout_ref, s_scr, prev_scr, *, ns, wave, nb, log_c, n_t):
    c = 1 << log_c
    rows = SLOT_ROWS
    n2 = 2 * rows
    pairs = A_HEADS // 2
    n_seq = ns * nb
    t_idx = pl.program_id(1)
    key_major = n_t > 1

    @pl.when(t_idx == 0)
    def _init():
        z = jnp.zeros((A_HEAD, A_HEAD), F32)
        for i in range(n_seq):
            for p in range(pairs):
                top = jnp.concatenate([s0_ref[i, 2 * p], z], axis=1)
                bot = jnp.concatenate([z, s0_ref[i, 2 * p + 1]], axis=1)
                s_pair = jnp.concatenate([top, bot], axis=0)
                s_scr[i, p] = s_pair.T if key_major else s_pair
        _load_prev(prev_scr, sh0_ref, 1)

    w0 = vec_ref[0:1, :]
    a0 = vec_ref[1:2, :]
    k_k = vec_ref[2:3, :]
    k_a = vec_ref[3:4, :]
    r_k = vec_ref[4:5, :]
    gn_g = vec_ref[5:6, :]
    gn_b = vec_ref[6:7, :]

    def head_sum(x, n_split):
        return _segment_sums(x, 6, n_split)

    strict, incl = _folded_masks(rows, log_c)
    lane0 = _iota((1, LANES), 1) < A_HEAD
    col_seq = jnp.bitwise_and(jnp.right_shift(_iota((1, n2), 1), log_c), nb - 1)

    def stack(xp):
        return jnp.concatenate([jnp.where(lane0, xp, 0.0), jnp.where(lane0, 0.0, xp)], axis=0)

    def seq_cols(xt, b):
        return xt if nb == 1 else jnp.where(col_seq == b, xt, 0.0)

    def pre(w):
        slot0 = w * wave
        seq0 = slot0 * nb
        u = pa_ref[slot0:slot0 + wave].reshape(wave * rows, A_SHIFT_W).astype(F32)
        prev = prev_scr[seq0 * SUBLANES:(seq0 + wave * nb) * SUBLANES, :]
        xs = u + (_shift_rows(u, 1, prev, c) - u) * mu_ref[...]
        _store_tails(u, c, 1, prev_scr, shout_ref, seq0)
        r = xs[:, 0:512]
        k = xs[:, 512:1024]
        v = xs[:, 1024:1536]
        wa = xs[:, 1536:1664]
        gd = xs[:, 1664:1792]
        wlora = _dot(jnp.tanh(wa).astype(BF16), ww2_ref[...])
        alora = _dot(wa.astype(BF16), wa2_ref[...])
        g = _dot(_sigmoid(gd).astype(BF16), wg2_ref[...])
        kkr = k * k_k
        kk_ss = head_sum(kkr * kkr, 1)
        yield
        wlog = -_softplus(-(w0 + wlora)) - 0.5
        lw = -jnp.exp(wlog)
        gcum = _mm_exact_lhs(_seq_cumsum_matrix(wave * rows, log_c), lw, 2)
        a = _sigmoid(a0 + alora)
        kk = kkr * lax.rsqrt(kk_ss + L2_EPS)
        k2 = k * (1.0 + (a - 1.0) * k_a)
        bonus = head_sum(r * k2 * r_k, 1) * v
        yield
        e_g = jnp.exp(gcum)
        e_gi = jnp.exp(-gcum)
        a_t = -kk * jnp.exp(gcum - lw)
        b_t = kk * a * e_gi
        k_t = k2 * e_gi
        r_t = r * e_g
        return a_t, b_t, k_t, r_t, v, e_g, bonus, g

    def work(w, prepared):
        a_t, b_t, k_t, r_t, v, e_g, bonus, g = prepared
        slot0 = w * wave
        chains = [(s, p) for s in range(wave) for p in range(pairs)]
        a_l, r_l, v_l, vt_l, b_l, k_l, n_l, ak_l, rb_l, rk_l, dec_l = ([] for _ in range(11))
        for s, p in chains:
            rs = slice(s * rows, (s + 1) * rows)
            sl = slice(p * LANES, (p + 1) * LANES)
            bk_s = jnp.concatenate([stack(b_t[rs, sl]), stack(k_t[rs, sl])], axis=0)
            bk_t = bk_s.T
            x = _mm(jnp.concatenate([a_t[rs, sl], r_t[rs, sl]], axis=0), bk_t)
            v_s = stack(v[rs, sl])
            a_l.append(a_t[rs, sl])
            r_l.append(r_t[rs, sl])
            v_l.append(v_s.astype(BF16))
            if key_major:
                b_l.append(bk_t[:, 0:n2])
                k_l.append(bk_t[:, n2:2 * n2])
            else:
                b_l.append(bk_s[0:n2].astype(BF16))
                k_l.append(bk_s[n2:2 * n2].astype(BF16))
                vt_l.append(v_s.T)
            n_l.append(jnp.where(strict, x[0:rows, 0:n2], 0.0))
            ak_l.append(jnp.where(strict, x[0:rows, n2:2 * n2], 0.0))
            rb_l.append(jnp.where(incl, x[rows:n2, 0:n2], 0.0))
            rk_l.append(jnp.where(incl, x[rows:n2, n2:2 * n2], 0.0))
            dec_l.append(e_g[rs, sl].T if key_major else e_g[rs, sl])
        yield
        tm_l = list(n_l)
        pw_l = list(n_l)
        pwb_l = [stack(q).astype(BF16) for q in pw_l]
        for _ in range(log_c - 1):
            pw_l = [_mm(q, qb) for q, qb in zip(pw_l, pwb_l)]
            pwb_l = [stack(q).astype(BF16) for q in pw_l]
            tm_l = [t + q + _mm(t, qb) for t, q, qb in zip(tm_l, pw_l, pwb_l)]
            yield
        akv_l, rkv_l, kv_l = [], [], []
        for ci in range(len(chains)):
            lhs = [ak_l[ci], rk_l[ci]] + ([seq_cols(k_l[ci], b) for b in range(nb)] if key_major else [])
            xv = _mm(jnp.concatenate(lhs, axis=0), v_l[ci])
            akv_l.append(xv[0:rows])
            rkv_l.append(xv[rows:n2])
            if key_major:
                kv = xv[n2:(nb + 1) * n2]
            else:
                kv = _mm(jnp.concatenate([seq_cols(vt_l[ci], b) for b in range(nb)], axis=0), k_l[ci])
            kv_l.append([kv[b * n2:(b + 1) * n2] for b in range(nb)])
        yield
        wu_l = []
        for ci in range(len(chains)):
            rhs = jnp.concatenate([a_l[ci], akv_l[ci]], axis=1)
            rhs_s = jnp.concatenate([stack(a_l[ci]), stack(akv_l[ci])], axis=1)
            wu_l.append(rhs + _mm(tm_l[ci], rhs_s))
        yield
        sa_l, rs_l = [], []
        for ci, (s, p) in enumerate(chains):
            sa_parts, rs_parts = [], []
            for b in range(nb):
                m_old = s_scr[(slot0 + s) * nb + b, p]
                wr = jnp.concatenate([wu_l[ci][b * c:(b + 1) * c, 0:LANES], r_l[ci][b * c:(b + 1) * c]], axis=0)
                xm = _mm(wr, m_old) if key_major else _mm_nt(wr, m_old)
                sa_parts.append(xm[0:c] + wu_l[ci][b * c:(b + 1) * c, LANES:2 * LANES])
                rs_parts.append(xm[c:2 * c])
            sa_l.append(sa_parts[0] if nb == 1 else jnp.concatenate(sa_parts, axis=0))
            rs_l.append(rs_parts[0] if nb == 1 else jnp.concatenate(rs_parts, axis=0))
        yield
        o_l = []
        for ci, (s, p) in enumerate(chains):
            sas = stack(sa_l[ci])
            if key_major:
                lhs = [seq_cols(b_l[ci], b) for b in range(nb)] + [rb_l[ci]]
                xs_ = _mm(jnp.concatenate(lhs, axis=0), sas)
                rb_sa = xs_[nb * n2:nb * n2 + rows]
            else:
                sas_t = sas.T
                xs_ = _mm(jnp.concatenate([seq_cols(sas_t, b) for b in range(nb)], axis=0), b_l[ci])
                rb_sa = _mm(rb_l[ci], sas)
            for b in range(nb):
                q = (slot0 + s) * nb + b
                upd = xs_[b * n2:(b + 1) * n2] + kv_l[ci][b]
                last = (b + 1) * c - 1
                decay = dec_l[ci][:, last:last + 1] if key_major else dec_l[ci][last:last + 1, :]
                s_scr[q, p] = (s_scr[q, p] + upd) * decay
            o_l.append(rs_l[ci] + rb_sa + rkv_l[ci])
        o_rows = [jnp.concatenate(o_l[s * pairs:(s + 1) * pairs], axis=1) for s in range(wave)]
        o = jnp.concatenate(o_rows, axis=0)
        yield
        mean = head_sum(o, 2) * (1.0 / A_HEAD)
        d = o - mean
        var = head_sum(d * d, 1) * (1.0 / A_HEAD)
        out = (d * lax.rsqrt(var + GN_EPS) * gn_g + gn_b + bonus) * g
        o_ref[slot0:slot0 + wave] = out.reshape(wave, rows, A_WIDTH)

    _run_waves(ns // wave, pre, work)

    @pl.when(t_idx == n_t - 1)
    def _fin():
        for i in range(n_seq):
            for p in range(pairs):
                s_pair = s_scr[i, p].T if key_major else s_scr[i, p]
                sfin_ref[i, 2 * p] = s_pair[0:A_HEAD, 0:A_HEAD]
                sfin_ref[i, 2 * p + 1] = s_pair[A_HEAD:LANES, A_HEAD:LANES]


def _mixer_specs(ns, nb, widths_in, width_out, state_block):
    n_seq = ns * nb
    data = lambda g, t: (g, t, 0)
    grp4 = lambda g, t: (g, 0, 0, 0)
    in_specs = ([pl.BlockSpec((ns, SLOT_ROWS, width), data) for width in widths_in]
                + [pl.BlockSpec((n_seq,) + state_block, grp4)])
    out_specs = [pl.BlockSpec((ns, SLOT_ROWS, width_out), data),
                 pl.BlockSpec((n_seq,) + state_block, grp4)]
    return in_specs, out_specs


def _rwkv_call(proj_a, shift0, wkv0, wts, ns, wave, nb, log_c):
    n_t = proj_a.shape[1] // SLOT_ROWS
    prev_rows = (SUBLANES if nb == 1 else SLOT_ROWS) * ns
    in_specs, out_specs = _mixer_specs(ns, nb, (A_SHIFT_W,), A_WIDTH, (A_HEADS, A_HEAD, A_HEAD))
    shift_spec = pl.BlockSpec((ns * nb, 1, A_SHIFT_W), lambda g, t: (g, 0, 0))
    kern = functools.partial(_rwkv_kernel, ns=ns, wave=wave, nb=nb, log_c=log_c, n_t=n_t)
    return pl.pallas_call(
        kern,
        grid=(proj_a.shape[0] // ns, n_t),
        in_specs=in_specs + [shift_spec] + [_const_spec(w.shape) for w in wts],
        out_specs=out_specs + [shift_spec],
        out_shape=[jax.ShapeDtypeStruct(proj_a.shape[:2] + (A_WIDTH,), F32),
                   jax.ShapeDtypeStruct(wkv0.shape, F32),
                   jax.ShapeDtypeStruct(shift0.shape, F32)],
        scratch_shapes=[pltpu.VMEM((ns * nb, A_HEADS // 2, LANES, LANES), F32),
                        pltpu.VMEM((prev_rows, A_SHIFT_W), F32)],
        compiler_params=pltpu.CompilerParams(
            dimension_semantics=("arbitrary", "arbitrary"), vmem_limit_bytes=VMEM_LIMIT_BYTES),
        name="rwkv7_mix",
    )(proj_a, wkv0, shift0, *wts)


def _gdn_kernel(pb_ref, pg_ref, s0_ref, hv_ref, o_ref, sfin_ref, s_scr, *, ns, wave, nb, log_c, n_t):
    c = 1 << log_c
    rows = SLOT_ROWS
    n2 = 2 * rows
    units = 2 * nb
    npairs = B_HEADS // 2
    t_idx = pl.program_id(1)

    @pl.when(t_idx == 0)
    def _init():
        s_scr[...] = s0_ref[...]

    a_log = hv_ref[0:1, :]
    dt_bias = hv_ref[1:2, :]
    norm_g = jnp.concatenate([hv_ref[2:3, :]] * B_HEADS, axis=1)

    strict, incl = _folded_masks(rows, log_c)
    lane0 = _iota((1, LANES), 1) < rows

    def stack(xf):
        return jnp.concatenate([jnp.where(lane0, xf, 0.0), jnp.where(lane0, 0.0, xf)], axis=0)

    row2 = _iota((n2, n2), 0)
    col2 = _iota((n2, n2), 1)
    last_sel_t = row2 == jnp.bitwise_or(col2, c - 1)
    col_unit = jnp.right_shift(_iota((1, n2), 1), log_c)

    def pre(w):
        slot0 = w * wave
        pb = pb_ref[slot0:slot0 + wave].reshape(wave * rows, B_MIX_W)
        ba = pg_ref[slot0:slot0 + wave].reshape(wave * rows, LANES)
        beta_t = _sigmoid(ba)
        g_t = -jnp.exp(a_log) * _softplus(ba + dt_bias)
        gcum_t = _mm_exact_lhs(_seq_cumsum_matrix(wave * rows, log_c), g_t, 3)
        yield
        return (pb[:, 0:B_WIDTH], pb[:, B_WIDTH:2 * B_WIDTH], pb[:, 2 * B_WIDTH:B_QKV], beta_t, gcum_t,
                pb[:, B_QKV:B_MIX_W].astype(F32))

    def state_index(slot, p, unit):
        return slot * nb + unit % nb, 2 * p + unit // nb

    def work(w, prepared):
        qn, kn, vv, beta_t, gcum_t, zs = prepared
        slot0 = w * wave
        chains = [(s, p) for s in range(wave) for p in range(npairs)]
        nl_l, rhs_l, attn_l, qdec_l, kdt_l, gl_l = ([] for _ in range(6))
        for s, p in chains:
            rs = slice(s * rows, (s + 1) * rows)
            hs = (2 * p, 2 * p + 1)
            q_s = jnp.concatenate([qn[rs, h * LANES:(h + 1) * LANES] for h in hs], axis=0)
            k_s = jnp.concatenate([kn[rs, h * LANES:(h + 1) * LANES] for h in hs], axis=0)
            v_s = jnp.concatenate([vv[rs, h * LANES:(h + 1) * LANES] for h in hs], axis=0)
            beta_s = jnp.broadcast_to(jnp.concatenate([beta_t[rs, h:h + 1] for h in hs], axis=0), (n2, LANES))
            g_s = jnp.concatenate([gcum_t[rs, B_HEADS + h:B_HEADS + h + 1] for h in hs], axis=0)
            gi = jnp.broadcast_to(g_s, (n2, n2))
            gj = gi.T
            gi_f = jnp.where(lane0, gi[0:rows], gi[rows:n2])
            dec_f = jnp.where(incl, jnp.exp(gi_f - gj[0:rows]), 0.0)
            kb = k_s * beta_s
            ks_t = k_s.T
            x = _mm(jnp.concatenate([kb, q_s], axis=0), ks_t)
            e_gs = jnp.exp(gi)
            g_last = jnp.sum(jnp.where(last_sel_t, gi, 0.0), axis=0, keepdims=True)
            l_f = jnp.where(lane0, x[0:rows], x[rows:n2])
            nl_l.append(-(l_f * jnp.where(strict, dec_f, 0.0)))
            attn_l.append((x[n2:2 * n2] * stack(dec_f)).astype(BF16))
            rhs_l.append(jnp.concatenate([v_s * beta_s, kb * e_gs], axis=1))
            qdec_l.append(q_s * e_gs)
            kdt_l.append(ks_t * jnp.exp(g_last - gj[0:1, :]))
            gl_l.append(jnp.exp(g_last))
        yield
        tm_l = list(nl_l)
        pw_l = list(nl_l)
        pwb_l = [stack(q).astype(BF16) for q in pw_l]
        for _ in range(log_c - 1):
            pw_l = [_mm(q, qb) for q, qb in zip(pw_l, pwb_l)]
            pwb_l = [stack(q).astype(BF16) for q in pw_l]
            tm_l = [t + q + _mm(t, qb) for t, q, qb in zip(tm_l, pw_l, pwb_l)]
            yield
        uw_l = [rhs + _mm(stack(tm), rhs) for tm, rhs in zip(tm_l, rhs_l)]
        yield
        vn_l, qs_l = [], []
        for ci, (s, p) in enumerate(chains):
            vn_parts, qs_parts = [], []
            for unit in range(units):
                s_old = s_scr[state_index(slot0 + s, p, unit)]
                vn_parts.append(uw_l[ci][unit * c:(unit + 1) * c, 0:LANES]
                                - _mm(uw_l[ci][unit * c:(unit + 1) * c, LANES:2 * LANES], s_old))
                qs_parts.append(_mm(qdec_l[ci][unit * c:(unit + 1) * c], s_old))
            vn_l.append(jnp.concatenate(vn_parts, axis=0).astype(BF16))
            qs_l.append(jnp.concatenate(qs_parts, axis=0))
        yield
        for ci, (s, p) in enumerate(chains):
            for unit in range(units):
                kd_u = jnp.where(col_unit == unit, kdt_l[ci], 0.0)
                idx = state_index(slot0 + s, p, unit)
                s_scr[idx] = s_scr[idx] * gl_l[ci][:, unit * c:unit * c + 1] + _mm(kd_u, vn_l[ci])
        o_rows = []
        for s in range(wave):
            heads = []
            for p in range(npairs):
                ci = s * npairs + p
                o_s = qs_l[ci] + _mm(attn_l[ci], vn_l[ci])
                heads += [o_s[0:rows], o_s[rows:n2]]
            o_rows.append(jnp.concatenate(heads, axis=1))
        o = jnp.concatenate(o_rows, axis=0)
        yield
        mean_sq = _segment_sums(o * o, 7, 1) * (1.0 / B_HEAD)
        out = o * lax.rsqrt(mean_sq + RMS_EPS) * norm_g * zs
        o_ref[slot0:slot0 + wave] = out.reshape(wave, rows, B_WIDTH)

    _run_waves(ns // wave, pre, work)

    @pl.when(t_idx == n_t - 1)
    def _fin():
        sfin_ref[...] = s_scr[...]


def _gdn_call(proj_b, proj_g, ssm0, wts, ns, wave, nb, log_c):
    n_t = proj_b.shape[1] // SLOT_ROWS
    in_specs, out_specs = _mixer_specs(ns, nb, (B_MIX_W, LANES), B_WIDTH, (B_HEADS, B_HEAD, B_HEAD))
    kern = functools.partial(_gdn_kernel, ns=ns, wave=wave, nb=nb, log_c=log_c, n_t=n_t)
    return pl.pallas_call(
        kern,
        grid=(proj_b.shape[0] // ns, n_t),
        in_specs=in_specs + [_const_spec(w.shape) for w in wts],
        out_specs=out_specs,
        out_shape=[jax.ShapeDtypeStruct(proj_b.shape[:2] + (B_WIDTH,), F32),
                   jax.ShapeDtypeStruct(ssm0.shape, F32)],
        scratch_shapes=[pltpu.VMEM((ns * nb, B_HEADS, B_HEAD, B_HEAD), F32)],
        compiler_params=pltpu.CompilerParams(
            dimension_semantics=("arbitrary", "arbitrary"), vmem_limit_bytes=VMEM_LIMIT_BYTES),
        name="gdn_mix",
    )(proj_b, proj_g, ssm0, *wts)


def _layer_norm(x, g, b):
    mu = jnp.mean(x, axis=-1, keepdims=True)
    d = x - mu
    var = jnp.mean(d * d, axis=-1, keepdims=True)
    return d * lax.rsqrt(var + LN_EPS) * g + b


def _post_kernel(oa_ref, ob_ref, x_ref, p_ref, fb_ref, woa_ref, wob_ref, lnv_ref, wgu_ref,
                 fv_ref, wd_ref, wple_ref, wpg_ref, y_ref, fout_ref, prev_scr, *, c):
    t_idx = pl.program_id(1)

    @pl.when(t_idx == 0)
    def _init():
        _load_prev(prev_scr, fb_ref, FFN_CONV - 1)

    ln1_g, ln1_b = lnv_ref[0:1, :], lnv_ref[1:2, :]
    ln2_g, ln2_b = lnv_ref[2:3, :], lnv_ref[3:4, :]
    ple_g = lnv_ref[4:5, :]

    mix = _dot(oa_ref[...].astype(BF16), woa_ref[...]) + _dot(ob_ref[...].astype(BF16), wob_ref[...])
    h = _layer_norm(DN_ALPHA * x_ref[...] + mix, ln1_g, ln1_b)
    hb = h.astype(BF16)
    ffn = None
    for f0, f1 in zip(FF_CUTS[:-1], FF_CUTS[1:]):
        fs = slice(f0, f1)
        gate = _dot(hb, wgu_ref[:, fs])
        up = _dot(hb, wgu_ref[:, D_FF + f0:D_FF + f1])
        prev = prev_scr[:, fs]
        gate_c = (fv_ref[0:1, fs] * _shift_rows(gate, 2, prev, c) + fv_ref[1:2, fs] * _shift_rows(gate, 1, prev, c)
                  + fv_ref[2:3, fs] * gate)
        _store_tails(gate, c, FFN_CONV - 1, prev_scr.at[:, fs], fout_ref.at[:, :, fs])
        act = _silu(gate_c + fv_ref[3:4, fs]) * up
        part = _dot(act.astype(BF16), wd_ref[fs, :])
        ffn = part if ffn is None else ffn + part
    h2 = _layer_norm(DN_ALPHA * h + ffn, ln2_g, ln2_b)
    pe = _dot(p_ref[...].astype(BF16), wple_ref[...])
    e = pe * lax.rsqrt(jnp.mean(pe * pe, axis=-1, keepdims=True) + RMS_EPS) * ple_g
    y_ref[...] = h2 + _sigmoid(_dot(h2.astype(BF16), wpg_ref[...])) * e


def _post_call(o_a, o_b, x2, p2, fconv0, wts, seq):
    rows = POST_ROWS[0] if seq % POST_ROWS[0] == 0 else POST_ROWS[1]
    c, nb, n_t, groups = _seq_blocking(rows, seq, fconv0.shape[0])
    prev_rows = SUBLANES if nb == 1 else rows
    row_map = lambda g, t: (g * n_t + t, 0)
    grp3 = lambda g, t: (g, 0, 0)
    kern = functools.partial(_post_kernel, c=c)
    return pl.pallas_call(
        kern,
        grid=(groups, n_t),
        in_specs=[pl.BlockSpec((rows, A_WIDTH), row_map),
                  pl.BlockSpec((rows, B_WIDTH), row_map),
                  pl.BlockSpec((rows, D_MODEL), row_map),
                  pl.BlockSpec((rows, PLE_DIM), row_map),
                  pl.BlockSpec((nb, FFN_CONV - 1, D_FF), grp3)]
                 + [_const_spec(w.shape) for w in wts],
        out_specs=[pl.BlockSpec((rows, D_MODEL), row_map),
                   pl.BlockSpec((nb, FFN_CONV - 1, D_FF), grp3)],
        out_shape=[jax.ShapeDtypeStruct(x2.shape, F32),
                   jax.ShapeDtypeStruct(fconv0.shape, F32)],
        scratch_shapes=[pltpu.VMEM((prev_rows, D_FF), F32)],
        compiler_params=pltpu.CompilerParams(
            dimension_semantics=("arbitrary", "arbitrary"), vmem_limit_bytes=VMEM_LIMIT_BYTES),
        name="post_mix_ffn",
    )(o_a, o_b, x2, p2, fconv0, *wts)


def _pad_rows(w, n):
    return jnp.pad(w, ((0, n - w.shape[0]), (0, 0)))


def _prepare_weights(w_in, a_mu, a_w0, a_w_w2, a_a0, a_w_a2, a_w_g2, a_k_k, a_k_a, a_r_k, a_gn_g,
                     a_gn_b, b_conv_w, b_a_log, b_dt_bias, b_norm_g, w_o, ln1_g, ln1_b, w_up,
                     f_conv_w, f_conv_b, w_down, ln2_g, ln2_b, w_ple, ple_g, w_ple_gate):
    lora = 64
    proj = (w_in[0].astype(BF16), _pad_rows(b_conv_w[0], SUBLANES))
    rwkv = (a_mu,
            _pad_rows(jnp.concatenate([a_w0, a_a0, a_k_k, a_k_a, a_r_k, a_gn_g, a_gn_b], axis=0), SUBLANES),
            _pad_rows(a_w_w2[0], LANES).astype(BF16),
            jnp.pad(a_w_a2[0], ((lora, LANES - 2 * lora), (0, 0))).astype(BF16),
            a_w_g2[0].astype(BF16))
    head_row = lambda vec: jnp.pad(vec, ((0, 0), (B_HEADS, LANES - 2 * B_HEADS)))
    gdn = (_pad_rows(jnp.concatenate([head_row(b_a_log), head_row(b_dt_bias), b_norm_g], axis=0), SUBLANES),)
    post = (w_o[0, :A_WIDTH].astype(BF16), w_o[0, A_WIDTH:].astype(BF16),
            _pad_rows(jnp.concatenate([ln1_g, ln1_b, ln2_g, ln2_b, ple_g], axis=0), SUBLANES),
            w_up[0].astype(BF16),
            _pad_rows(jnp.concatenate([f_conv_w[0], f_conv_b], axis=0), SUBLANES),
            w_down[0].astype(BF16), w_ple[0].astype(BF16), w_ple_gate[0].astype(BF16))
    return proj, rwkv, gdn, post


def _trunk(x, p, states, weights):
    proj_w, rwkv_w, gdn_w, post_w = weights
    a_wkv, a_shift, b_ssm, b_conv, f_conv = (s[0] for s in states)
    bsz, seq, _ = x.shape
    m = bsz * seq
    x2 = x.reshape(m, D_MODEL)
    p2 = p.reshape(m, PLE_DIM)
    if seq % SLOT_ROWS == 0:
        nb, log_c, slots, kind = 1, 6, bsz, 0
    else:
        assert seq == SUBLANES
        nb, log_c, slots, kind = SLOT_ROWS // seq, 3, m // SLOT_ROWS, 1
    proj_a, proj_b, proj_g, conv_new = _proj_call(x2, b_conv, *proj_w, seq)
    o_a, wkv_new, shift_new = _rwkv_call(proj_a.reshape(slots, -1, A_SHIFT_W), a_shift[:, None, :], a_wkv,
                                         rwkv_w, *RWKV_SLOTS[kind], nb, log_c)
    o_b, ssm_new = _gdn_call(proj_b.reshape(slots, -1, B_MIX_W), proj_g.reshape(slots, -1, LANES), b_ssm,
                             gdn_w, *GDN_SLOTS[kind], nb, log_c)
    y2, fconv_new = _post_call(o_a.reshape(m, A_WIDTH), o_b.reshape(m, B_WIDTH), x2, p2, f_conv, post_w, seq)
    new_states = (wkv_new, shift_new[:, 0, :], ssm_new, conv_new, fconv_new)
    return y2.reshape(bsz, seq, D_MODEL), tuple(s[None] for s in new_states)


def kernel(x_prompt, x_sample, p_prompt, p_sample, state_a_wkv, state_a_shift, state_b_ssm, state_b_conv, state_ffn_conv, w_in, a_mu, a_w0, a_w_w2, a_a0, a_w_a2, a_w_g2, a_k_k, a_k_a, a_r_k, a_gn_g, a_gn_b, b_conv_w, b_a_log, b_dt_bias, b_norm_g, w_o, ln1_g, ln1_b, w_up, f_conv_w, f_conv_b, w_down, ln2_g, ln2_b, w_ple, ple_g, w_ple_gate):
    assert w_in.shape[0] == DEPTH
    weights = _prepare_weights(w_in, a_mu, a_w0, a_w_w2, a_a0, a_w_a2, a_w_g2, a_k_k, a_k_a, a_r_k,
                               a_gn_g, a_gn_b, b_conv_w, b_a_log, b_dt_bias, b_norm_g, w_o, ln1_g,
                               ln1_b, w_up, f_conv_w, f_conv_b, w_down, ln2_g, ln2_b, w_ple, ple_g,
                               w_ple_gate)
    bp = x_prompt.shape[0]
    zeros = lambda *s: jnp.zeros((DEPTH, bp) + s, x_prompt.dtype)
    prompt_init = (zeros(A_HEADS, A_HEAD, A_HEAD), zeros(A_SHIFT_W), zeros(B_HEADS, B_HEAD, B_HEAD),
                   zeros(B_CONV - 1, B_QKV), zeros(FFN_CONV - 1, D_FF))
    y_prompt, prompt_states = _trunk(x_prompt, p_prompt[0], prompt_init, weights)
    sample_init = (state_a_wkv, state_a_shift, state_b_ssm, state_b_conv, state_ffn_conv)
    y_sample, sample_states = _trunk(x_sample, p_sample[0], sample_init, weights)
    return (y_prompt, y_sample) + prompt_states + sample_states
```

```python
import functools
import math

import jax
import jax.numpy as jnp
from jax import lax
from jax.experimental import pallas as pl
from jax.experimental.pallas import tpu as pltpu

F32 = jnp.float32
BF16 = jnp.bfloat16

LANES = 128
SUBLANES = 8
MXU_TILE = 256
VMEM_LIMIT_BYTES = 56 * 1024 * 1024

D_MODEL = 1024
A_WIDTH = 512
A_HEAD = 64
LOG_A_HEAD = 6
A_HEADS = 8
A_SHIFT_W = 1792
B_WIDTH = 512
B_HEAD = 128
LOG_B_HEAD = 7
B_HEADS = 4
B_CONV = 4
B_QKV = 1536
B_MIX_W = 2048
D_FF = 2816
FFN_CONV = 3
PLE_DIM = 256
DEPTH = 1
DN_ALPHA = (2.0 * DEPTH) ** 0.25
LN_EPS = 1e-5
GN_EPS = 64e-5
RMS_EPS = 1e-6
L2_EPS = 1e-12

SLOT_ROWS = 64
POST_ROWS = (512, 256)
PROJ_ROWS = 512
FF_CUTS = (0, 1536, 2816)
MIXER_SLOTS = ((8, (4, 8)), (2, (2, 2)))


def _dot(a, b):
    return jnp.dot(a, b, preferred_element_type=F32)


def _mm(a, b):
    return _dot(a.astype(BF16), b.astype(BF16))


def _split(x, n):
    parts = []
    r = x
    for i in range(n):
        h = r.astype(BF16)
        parts.append(h)
        if i + 1 < n:
            r = r - h.astype(F32)
    return parts


def _mm_nt(a, b):
    return lax.dot_general(a.astype(BF16), b.astype(BF16), (((1,), (1,)), ((), ())), preferred_element_type=F32)


def _mm_exact_lhs(m, x, n):
    out = None
    for part in _split(x, n):
        d = _dot(m, part)
        out = d if out is None else out + d
    return out


def _mm_exact_rhs(x, m, n):
    out = None
    for part in _split(x, n):
        d = _dot(part, m)
        out = d if out is None else out + d
    return out


def _segment_sums(x, log_seg, n_split):
    tile = min(MXU_TILE, x.shape[1])
    same = (jnp.right_shift(_iota((tile, tile), 0), log_seg) == jnp.right_shift(_iota((tile, tile), 1), log_seg))
    ones = jnp.where(same, 1.0, 0.0).astype(BF16)
    parts = [_mm_exact_rhs(x[:, s:s + tile], ones, n_split) for s in range(0, x.shape[1], tile)]
    return parts[0] if len(parts) == 1 else jnp.concatenate(parts, axis=1)


def _sigmoid(x):
    return 1.0 / (1.0 + jnp.exp2(x * (-math.log2(math.e))))


def _silu(x):
    return x * _sigmoid(x)


def _softplus(x):
    return jnp.maximum(x, 0.0) + jnp.log1p(jnp.exp(-jnp.abs(x)))


def _iota(shape, dim):
    return lax.broadcasted_iota(jnp.int32, shape, dim)


def _shift_rows(x, j, prev, c):
    rows = x.shape[0]
    rolled = pltpu.roll(x, j, 0)
    if c == SUBLANES:
        t = jnp.bitwise_and(_iota((rows, 1), 0), SUBLANES - 1)
        return jnp.where(t >= j, rolled, pltpu.roll(prev, rows - SUBLANES + j, 0))
    t = _iota((SUBLANES, 1), 0)
    parts = []
    for i in range(rows // c):
        head = jnp.where(t >= j, rolled[i * c:i * c + SUBLANES],
                         pltpu.roll(prev[i * SUBLANES:(i + 1) * SUBLANES], j, 0))
        parts += [head, rolled[i * c + SUBLANES:(i + 1) * c]]
    return jnp.concatenate(parts, axis=0)


def _load_prev(prev_scr, buf_ref, k_prev):
    prev_scr[...] = jnp.zeros(prev_scr.shape, F32)
    for i in range(buf_ref.shape[0]):
        prev_scr[SUBLANES * (i + 1) - k_prev:SUBLANES * (i + 1), :] = buf_ref[i]


def _store_tails(x, c, k_prev, prev_scr, tail_ref, seq0=0):
    for i in range(x.shape[0] // c):
        q = seq0 + i
        tail_ref[q] = x[(i + 1) * c - k_prev:(i + 1) * c, :]
        if c > SUBLANES:
            prev_scr[q * SUBLANES:(q + 1) * SUBLANES, :] = x[(i + 1) * c - SUBLANES:(i + 1) * c, :]


def _waves(n_waves, pre, work):
    ready = yield from pre(0)
    for w in range(n_waves):
        cur = work(w, ready)
        nxt = pre(w + 1) if w + 1 < n_waves else None
        cur_done = False
        while not cur_done or nxt is not None:
            if not cur_done:
                try:
                    next(cur)
                except StopIteration:
                    cur_done = True
            if nxt is not None:
                try:
                    next(nxt)
                except StopIteration as stop:
                    ready = stop.value
                    nxt = None
            yield


def _alternate(*gens):
    gens = list(gens)
    while gens:
        for gen in list(gens):
            try:
                next(gen)
            except StopIteration:
                gens.remove(gen)


def _seq_cumsum_matrix(rows, log_c):
    r = _iota((rows, rows), 0)
    q = _iota((rows, rows), 1)
    same = jnp.right_shift(r, log_c) == jnp.right_shift(q, log_c)
    return jnp.where(jnp.logical_and(same, r >= q), 1.0, 0.0).astype(BF16)


def _folded_masks(rows, log_c):
    r = _iota((rows, 2 * rows), 0)
    q = jnp.bitwise_and(_iota((rows, 2 * rows), 1), rows - 1)
    same = jnp.right_shift(r, log_c) == jnp.right_shift(q, log_c)
    return jnp.logical_and(same, r > q), jnp.logical_and(same, r >= q)


def _proj_kernel(xl_ref, xs_ref, w_ref, cbl_ref, cbs_ref, cw_ref,
                 pal_ref, pbl_ref, pgl_ref, coutl_ref, pas_ref, pbs_ref, pgs_ref, couts_ref,
                 prevl_scr, prevs_scr, *, long_steps, steps_per_seq, c_long, c_short):
    i = pl.program_id(0)

    @pl.when(i < long_steps)
    def _long():
        _proj_block(xl_ref, w_ref, cbl_ref, cw_ref, pal_ref, pbl_ref, pgl_ref, coutl_ref, prevl_scr,
                    c=c_long, first=lax.rem(i, steps_per_seq) == 0)

    @pl.when(i >= long_steps)
    def _short():
        _proj_block(xs_ref, w_ref, cbs_ref, cw_ref, pas_ref, pbs_ref, pgs_ref, couts_ref, prevs_scr,
                    c=c_short, first=i >= long_steps)


def _proj_block(x_ref, w_ref, cb_ref, cw_ref, pa_ref, pb_ref, pg_ref, cout_ref, prev_scr, *, c, first):
    @pl.when(first)
    def _init():
        _load_prev(prev_scr, cb_ref, B_CONV - 1)

    xb = x_ref[...].astype(BF16)
    in_w = w_ref.shape[1]
    gates_w = in_w - A_SHIFT_W - B_QKV - B_WIDTH
    pb = _dot(xb, w_ref[:, A_SHIFT_W:in_w])
    x = pb[:, 0:B_QKV]
    prev = prev_scr[...]
    conv = (cw_ref[0:1, :] * _shift_rows(x, 3, prev, c) + cw_ref[1:2, :] * _shift_rows(x, 2, prev, c)
            + cw_ref[2:3, :] * _shift_rows(x, 1, prev, c) + cw_ref[3:4, :] * x)
    _store_tails(x, c, B_CONV - 1, prev_scr, cout_ref)
    qkv = _silu(conv)
    qk = qkv[:, 0:2 * B_WIDTH]
    qk = qk * lax.rsqrt(_segment_sums(qk * qk, LOG_B_HEAD, 1) + L2_EPS)
    pb_ref[:, 0:B_WIDTH] = (qk[:, 0:B_WIDTH] * (B_HEAD ** -0.5)).astype(BF16)
    pb_ref[:, B_WIDTH:2 * B_WIDTH] = qk[:, B_WIDTH:2 * B_WIDTH].astype(BF16)
    pb_ref[:, 2 * B_WIDTH:B_QKV] = qkv[:, 2 * B_WIDTH:B_QKV].astype(BF16)
    pb_ref[:, B_QKV:B_QKV + B_WIDTH] = _silu(pb[:, B_QKV:B_QKV + B_WIDTH]).astype(BF16)
    pg_ref[:, 0:gates_w] = pb[:, B_QKV + B_WIDTH:B_QKV + B_WIDTH + gates_w]
    pg_ref[:, gates_w:LANES] = jnp.zeros((pg_ref.shape[0], LANES - gates_w), F32)
    pa_ref[...] = _dot(xb, w_ref[:, 0:A_SHIFT_W]).astype(BF16)


def _const_spec(shape):
    zeros = (0,) * len(shape)
    return pl.BlockSpec(shape, lambda *_: zeros, pipeline_mode=pl.Buffered(1))


def _seq_blocking(rows, seq, n_seq):
    c = rows if seq % rows == 0 else seq
    nb = rows // c
    return c, nb, seq // c, n_seq // nb


def _proj_call(x_long, conv_long, seq_long, x_short, conv_short, seq_short, w_in, conv_w):
    rows = PROJ_ROWS
    c_long, nb_long, n_t, groups = _seq_blocking(rows, seq_long, conv_long.shape[0])
    c_short, nb_short, short_t, short_steps = _seq_blocking(rows, seq_short, conv_short.shape[0])
    assert nb_long == 1 and short_t == 1
    long_steps = groups * n_t
    long_blk = lambda i: jnp.minimum(i, long_steps - 1)
    short_blk = lambda i: jnp.maximum(i - long_steps, 0)

    def specs(blk, nb, per_seq):
        row_map = lambda i: (blk(i), 0)
        return ([pl.BlockSpec((rows, w), row_map) for w in (D_MODEL, A_SHIFT_W, B_MIX_W, LANES)],
                pl.BlockSpec((nb, B_CONV - 1, B_QKV), lambda i: (blk(i) // per_seq, 0, 0)))

    def shapes(m, conv0):
        return [jax.ShapeDtypeStruct((m, A_SHIFT_W), BF16), jax.ShapeDtypeStruct((m, B_MIX_W), BF16),
                jax.ShapeDtypeStruct((m, LANES), F32), jax.ShapeDtypeStruct(conv0.shape, F32)]

    (xl_spec, *outl_specs), cl_spec = specs(long_blk, nb_long, n_t)
    (xs_spec, *outs_specs), cs_spec = specs(short_blk, nb_short, 1)
    outs = pl.pallas_call(
        functools.partial(_proj_kernel, long_steps=long_steps, steps_per_seq=n_t, c_long=c_long, c_short=c_short),
        grid=(long_steps + short_steps,),
        in_specs=[xl_spec, xs_spec, _const_spec(w_in.shape), cl_spec, cs_spec, _const_spec(conv_w.shape)],
        out_specs=outl_specs + [cl_spec] + outs_specs + [cs_spec],
        out_shape=shapes(x_long.shape[0], conv_long) + shapes(x_short.shape[0], conv_short),
        scratch_shapes=[pltpu.VMEM((SUBLANES, B_QKV), F32), pltpu.VMEM((rows, B_QKV), F32)],
        compiler_params=pltpu.CompilerParams(
            dimension_semantics=("arbitrary",), vmem_limit_bytes=VMEM_LIMIT_BYTES),
        name="in_proj",
    )(x_long, x_short, w_in, conv_long, conv_short, conv_w)
    return outs[0:4], outs[4:8]


def _rwkv_parts(pa_ref, s0_ref, sh0_ref, mu_ref, vec_ref, ww2_ref, wa2_ref, wg2_ref,
                o_ref, sfin_ref, shout_ref, s_scr, prev_scr, *, ns, wave, nb, log_c, n_t):
    c = 1 << log_c
    rows = SLOT_ROWS
    n2 = 2 * rows
    pairs = A_HEADS // 2
    n_seq = ns * nb
    t_idx = pl.program_id(1)
    key_major = n_t > 1

    @pl.when(t_idx == 0)
    def _init():
        z = jnp.zeros((A_HEAD, A_HEAD), F32)
        for i in range(n_seq):
            for p in range(pairs):
                top = jnp.concatenate([s0_ref[i, 2 * p], z], axis=1)
                bot = jnp.concatenate([z, s0_ref[i, 2 * p + 1]], axis=1)
                s_pair = jnp.concatenate([top, bot], axis=0)
                s_scr[i, p] = s_pair.T if key_major else s_pair
        _load_prev(prev_scr, sh0_ref, 1)

    w0 = vec_ref[0:1, :]
    a0 = vec_ref[1:2, :]
    k_k = vec_ref[2:3, :]
    k_a = vec_ref[3:4, :]
    r_k = vec_ref[4:5, :]
    gn_g = vec_ref[5:6, :]
    gn_b = vec_ref[6:7, :]

    def head_sum(x, n_split):
        return _segment_sums(x, LOG_A_HEAD, n_split)

    strict, incl = _folded_masks(rows, log_c)
    lane0 = _iota((1, LANES), 1) < A_HEAD
    col_seq = jnp.bitwise_and(jnp.right_shift(_iota((1, n2), 1), log_c), nb - 1)

    def stack(xp):
        return jnp.concatenate([jnp.where(lane0, xp, 0.0), jnp.where(lane0, 0.0, xp)], axis=0)

    def seq_cols(xt, b):
        return xt if nb == 1 else jnp.where(col_seq == b, xt, 0.0)

    def pre(w):
        slot0 = w * wave
        seq0 = slot0 * nb
        u = pa_ref[slot0:slot0 + wave].reshape(wave * rows, A_SHIFT_W).astype(F32)
        prev = prev_scr[seq0 * SUBLANES:(seq0 + wave * nb) * SUBLANES, :]
        xs = u + (_shift_rows(u, 1, prev, c) - u) * mu_ref[...]
        _store_tails(u, c, 1, prev_scr, shout_ref, seq0)
        r = xs[:, 0:A_WIDTH]
        k = xs[:, A_WIDTH:2 * A_WIDTH]
        v = xs[:, 2 * A_WIDTH:3 * A_WIDTH]
        wa = xs[:, 3 * A_WIDTH:3 * A_WIDTH + LANES]
        gd = xs[:, 3 * A_WIDTH + LANES:A_SHIFT_W]
        wlora = _dot(jnp.tanh(wa).astype(BF16), ww2_ref[...])
        alora = _dot(wa.astype(BF16), wa2_ref[...])
        g = _dot(_sigmoid(gd).astype(BF16), wg2_ref[...])
        kkr = k * k_k
        kk_ss = head_sum(kkr * kkr, 1)
        yield
        lw = -math.exp(-0.5) * _sigmoid(w0 + wlora)
        gcum = _mm_exact_lhs(_seq_cumsum_matrix(wave * rows, log_c), lw, 2)
        a = _sigmoid(a0 + alora)
        kk = kkr * lax.rsqrt(kk_ss + L2_EPS)
        k2 = k * (1.0 + (a - 1.0) * k_a)
        bonus = head_sum(r * k2 * r_k, 1) * v
        yield
        e_g = jnp.exp(gcum)
        e_gi = jnp.exp(-gcum)
        a_t = -kk * jnp.exp(gcum - lw)
        b_t = kk * a * e_gi
        k_t = k2 * e_gi
        r_t = r * e_g
        return a_t, b_t, k_t, r_t, v, e_g, bonus, g

    def work(w, prepared):
        a_t, b_t, k_t, r_t, v, e_g, bonus, g = prepared
        slot0 = w * wave
        chains = [(s, p) for s in range(wave) for p in range(pairs)]
        a_l, r_l, v_l, vt_l, b_l, k_l, n_l, ak_l, rb_l, rk_l, dec_l = ([] for _ in range(11))
        for s, p in chains:
            rs = slice(s * rows, (s + 1) * rows)
            sl = slice(p * LANES, (p + 1) * LANES)
            bk_s = jnp.concatenate([stack(b_t[rs, sl]), stack(k_t[rs, sl])], axis=0)
            bk_t = bk_s.T
            x = _mm(jnp.concatenate([a_t[rs, sl], r_t[rs, sl]], axis=0), bk_t)
            v_s = stack(v[rs, sl])
            a_l.append(a_t[rs, sl])
            r_l.append(r_t[rs, sl])
            v_l.append(v_s.astype(BF16))
            if key_major:
                b_l.append(bk_t[:, 0:n2])
                k_l.append(bk_t[:, n2:2 * n2])
            else:
                b_l.append(bk_s[0:n2].astype(BF16))
                k_l.append(bk_s[n2:2 * n2].astype(BF16))
                vt_l.append(v_s.T)
            n_l.append(jnp.where(strict, x[0:rows, 0:n2], 0.0))
            ak_l.append(jnp.where(strict, x[0:rows, n2:2 * n2], 0.0))
            rb_l.append(jnp.where(incl, x[rows:n2, 0:n2], 0.0))
            rk_l.append(jnp.where(incl, x[rows:n2, n2:2 * n2], 0.0))
            dec_l.append(e_g[rs, sl].T if key_major else e_g[rs, sl])
        yield
        tm_l = list(n_l)
        pw_l = list(n_l)
        pwb_l = [stack(q).astype(BF16) for q in pw_l]
        for _ in range(log_c - 1):
            pw_l = [_mm(q, qb) for q, qb in zip(pw_l, pwb_l)]
            pwb_l = [stack(q).astype(BF16) for q in pw_l]
            tm_l = [t + q + _mm(t, qb) for t, q, qb in zip(tm_l, pw_l, pwb_l)]
            yield
        akv_l, rkv_l, kv_l = [], [], []
        for ci in range(len(chains)):
            lhs = [ak_l[ci], rk_l[ci]] + ([seq_cols(k_l[ci], b) for b in range(nb)] if key_major else [])
            xv = _mm(jnp.concatenate(lhs, axis=0), v_l[ci])
            akv_l.append(xv[0:rows])
            rkv_l.append(xv[rows:n2])
            if key_major:
                kv = xv[n2:(nb + 1) * n2]
            else:
                kv = _mm(jnp.concatenate([seq_cols(vt_l[ci], b) for b in range(nb)], axis=0), k_l[ci])
            kv_l.append([kv[b * n2:(b + 1) * n2] for b in range(nb)])
        yield
        wu_l = []
        for ci in range(len(chains)):
            rhs = jnp.concatenate([a_l[ci], akv_l[ci]], axis=1)
            rhs_s = jnp.concatenate([stack(a_l[ci]), stack(akv_l[ci])], axis=1)
            wu_l.append(rhs + _mm(tm_l[ci], rhs_s))
        yield
        sa_l, rs_l = [], []
        for ci, (s, p) in enumerate(chains):
            sa_parts, rs_parts = [], []
            for b in range(nb):
                m_old = s_scr[(slot0 + s) * nb + b, p]
                wr = jnp.concatenate([wu_l[ci][b * c:(b + 1) * c, 0:LANES], r_l[ci][b * c:(b + 1) * c]], axis=0)
                xm = _mm(wr, m_old) if key_major else _mm_nt(wr, m_old)
                sa_parts.append(xm[0:c] + wu_l[ci][b * c:(b + 1) * c, LANES:2 * LANES])
                rs_parts.append(xm[c:2 * c])
            sa_l.append(sa_parts[0] if nb == 1 else jnp.concatenate(sa_parts, axis=0))
            rs_l.append(rs_parts[0] if nb == 1 else jnp.concatenate(rs_parts, axis=0))
        yield
        o_l = []
        for ci, (s, p) in enumerate(chains):
            sas = stack(sa_l[ci])
            if key_major:
                lhs = [seq_cols(b_l[ci], b) for b in range(nb)] + [rb_l[ci]]
                xs_ = _mm(jnp.concatenate(lhs, axis=0), sas)
                rb_sa = xs_[nb * n2:nb * n2 + rows]
            else:
                sas_t = sas.T
                xs_ = _mm(jnp.concatenate([seq_cols(sas_t, b) for b in range(nb)], axis=0), b_l[ci])
                rb_sa = _mm(rb_l[ci], sas)
            for b in range(nb):
                q = (slot0 + s) * nb + b
                upd = xs_[b * n2:(b + 1) * n2] + kv_l[ci][b]
                last = (b + 1) * c - 1
                decay = dec_l[ci][:, last:last + 1] if key_major else dec_l[ci][last:last + 1, :]
                s_scr[q, p] = (s_scr[q, p] + upd) * decay
            o_l.append(rs_l[ci] + rb_sa + rkv_l[ci])
        o_rows = [jnp.concatenate(o_l[s * pairs:(s + 1) * pairs], axis=1) for s in range(wave)]
        o = jnp.concatenate(o_rows, axis=0)
        yield
        mean = head_sum(o, 2) * (1.0 / A_HEAD)
        d = o - mean
        var = head_sum(d * d, 1) * (1.0 / A_HEAD)
        out = (d * lax.rsqrt(var + GN_EPS) * gn_g + gn_b + bonus) * g
        o_ref[slot0:slot0 + wave] = out.reshape(wave, rows, A_WIDTH).astype(o_ref.dtype)

    def finish():
        @pl.when(t_idx == n_t - 1)
        def _fin():
            for i in range(n_seq):
                for p in range(pairs):
                    s_pair = s_scr[i, p].T if key_major else s_scr[i, p]
                    sfin_ref[i, 2 * p] = s_pair[0:A_HEAD, 0:A_HEAD]
                    sfin_ref[i, 2 * p + 1] = s_pair[A_HEAD:LANES, A_HEAD:LANES]

    return ns // wave, pre, work, finish


def _gdn_parts(pb_ref, pg_ref, s0_ref, hv_ref, o_ref, sfin_ref, s_scr, *, ns, wave, nb, log_c, n_t):
    c = 1 << log_c
    rows = SLOT_ROWS
    n2 = 2 * rows
    units = 2 * nb
    npairs = B_HEADS // 2
    t_idx = pl.program_id(1)

    @pl.when(t_idx == 0)
    def _init():
        s_scr[...] = s0_ref[...]

    a_log = hv_ref[0:1, :]
    dt_bias = hv_ref[1:2, :]
    norm_g = jnp.concatenate([hv_ref[2:3, :]] * B_HEADS, axis=1)

    strict, incl = _folded_masks(rows, log_c)
    lane0 = _iota((1, LANES), 1) < rows

    def stack(xf):
        return jnp.concatenate([jnp.where(lane0, xf, 0.0), jnp.where(lane0, 0.0, xf)], axis=0)

    row2 = _iota((n2, n2), 0)
    col2 = _iota((n2, n2), 1)
    last_sel_t = row2 == jnp.bitwise_or(col2, c - 1)
    col_unit = jnp.right_shift(_iota((1, n2), 1), log_c)

    def pre(w):
        slot0 = w * wave
        pb = pb_ref[slot0:slot0 + wave].reshape(wave * rows, B_MIX_W)
        ba = pg_ref[slot0:slot0 + wave].reshape(wave * rows, LANES)
        beta_t = _sigmoid(ba)
        g_t = -jnp.exp(a_log) * _softplus(ba + dt_bias)
        gcum_t = _mm_exact_lhs(_seq_cumsum_matrix(wave * rows, log_c), g_t, 3)
        yield
        return (pb[:, 0:B_WIDTH], pb[:, B_WIDTH:2 * B_WIDTH], pb[:, 2 * B_WIDTH:B_QKV], beta_t, gcum_t,
                pb[:, B_QKV:B_MIX_W].astype(F32))

    def state_index(slot, p, unit):
        return slot * nb + unit % nb, 2 * p + unit // nb

    def work(w, prepared):
        qn, kn, vv, beta_t, gcum_t, zs = prepared
        slot0 = w * wave
        chains = [(s, p) for s in range(wave) for p in range(npairs)]
        nl_l, rhs_l, attn_l, qdec_l, kdt_l, gl_l = ([] for _ in range(6))
        for s, p in chains:
            rs = slice(s * rows, (s + 1) * rows)
            hs = (2 * p, 2 * p + 1)
            q_s = jnp.concatenate([qn[rs, h * LANES:(h + 1) * LANES] for h in hs], axis=0)
            k_s = jnp.concatenate([kn[rs, h * LANES:(h + 1) * LANES] for h in hs], axis=0)
            v_s = jnp.concatenate([vv[rs, h * LANES:(h + 1) * LANES] for h in hs], axis=0)
            beta_s = jnp.broadcast_to(jnp.concatenate([beta_t[rs, h:h + 1] for h in hs], axis=0), (n2, LANES))
            g_s = jnp.concatenate([gcum_t[rs, B_HEADS + h:B_HEADS + h + 1] for h in hs], axis=0)
            gi = jnp.broadcast_to(g_s, (n2, n2))
            gj = gi.T
            gi_f = jnp.where(lane0, gi[0:rows], gi[rows:n2])
            dec_f = jnp.where(incl, jnp.exp(gi_f - gj[0:rows]), 0.0)
            kb = k_s * beta_s
            ks_t = k_s.T
            x = _mm(jnp.concatenate([kb, q_s], axis=0), ks_t)
            e_gs = jnp.exp(gi)
            g_last = jnp.sum(jnp.where(last_sel_t, gi, 0.0), axis=0, keepdims=True)
            l_f = jnp.where(lane0, x[0:rows], x[rows:n2])
            nl_l.append(-(l_f * jnp.where(strict, dec_f, 0.0)))
            attn_l.append((x[n2:2 * n2] * stack(dec_f)).astype(BF16))
            rhs_l.append(jnp.concatenate([v_s * beta_s, kb * e_gs], axis=1))
            qdec_l.append(q_s * e_gs)
            kdt_l.append(ks_t * jnp.exp(g_last - gj[0:1, :]))
            gl_l.append(jnp.exp(g_last))
        yield
        tm_l = list(nl_l)
        pw_l = list(nl_l)
        pwb_l = [stack(q).astype(BF16) for q in pw_l]
        for _ in range(log_c - 1):
            pw_l = [_mm(q, qb) for q, qb in zip(pw_l, pwb_l)]
            pwb_l = [stack(q).astype(BF16) for q in pw_l]
            tm_l = [t + q + _mm(t, qb) for t, q, qb in zip(tm_l, pw_l, pwb_l)]
            yield
        uw_l = [rhs + _mm(stack(tm), rhs) for tm, rhs in zip(tm_l, rhs_l)]
        yield
        vn_l, qs_l = [], []
        for ci, (s, p) in enumerate(chains):
            vn_parts, qs_parts = [], []
            for unit in range(units):
                s_old = s_scr[state_index(slot0 + s, p, unit)]
                us = slice(unit * c, (unit + 1) * c)
                vn_parts.append(uw_l[ci][us, 0:LANES] - _mm(uw_l[ci][us, LANES:2 * LANES], s_old))
                qs_parts.append(_mm(qdec_l[ci][us], s_old))
            vn_l.append(jnp.concatenate(vn_parts, axis=0).astype(BF16))
            qs_l.append(jnp.concatenate(qs_parts, axis=0))
        yield
        for ci, (s, p) in enumerate(chains):
            for unit in range(units):
                kd_u = jnp.where(col_unit == unit, kdt_l[ci], 0.0)
                idx = state_index(slot0 + s, p, unit)
                s_scr[idx] = s_scr[idx] * gl_l[ci][:, unit * c:unit * c + 1] + _mm(kd_u, vn_l[ci])
        o_rows = []
        for s in range(wave):
            heads = []
            for p in range(npairs):
                ci = s * npairs + p
                o_s = qs_l[ci] + _mm(attn_l[ci], vn_l[ci])
                heads += [o_s[0:rows], o_s[rows:n2]]
            o_rows.append(jnp.concatenate(heads, axis=1))
        o = jnp.concatenate(o_rows, axis=0)
        yield
        mean_sq = _segment_sums(o * o, LOG_B_HEAD, 1) * (1.0 / B_HEAD)
        out = o * lax.rsqrt(mean_sq + RMS_EPS) * norm_g * zs
        o_ref[slot0:slot0 + wave] = out.reshape(wave, rows, B_WIDTH).astype(o_ref.dtype)

    def finish():
        @pl.when(t_idx == n_t - 1)
        def _fin():
            sfin_ref[...] = s_scr[...]

    return ns // wave, pre, work, finish


def _mixers_kernel(pa_ref, pb_ref, pg_ref, wkv0_ref, sh0_ref, ssm0_ref, mu_ref, vec_ref, ww2_ref, wa2_ref,
                   wg2_ref, hv_ref, oa_ref, ob_ref, wkv_ref, shout_ref, ssm_ref, wkv_scr, prev_scr, ssm_scr,
                   *, ns, waves, nb, log_c, n_t):
    common = dict(ns=ns, nb=nb, log_c=log_c, n_t=n_t)
    a_waves, a_pre, a_work, a_finish = _rwkv_parts(
        pa_ref, wkv0_ref, sh0_ref, mu_ref, vec_ref, ww2_ref, wa2_ref, wg2_ref, oa_ref, wkv_ref, shout_ref,
        wkv_scr, prev_scr, wave=waves[0], **common)
    b_waves, b_pre, b_work, b_finish = _gdn_parts(
        pb_ref, pg_ref, ssm0_ref, hv_ref, ob_ref, ssm_ref, ssm_scr, wave=waves[1], **common)
    _alternate(_waves(a_waves, a_pre, a_work), _waves(b_waves, b_pre, b_work))
    a_finish()
    b_finish()


def _mixers_call(proj_a, proj_b, proj_g, shift0, wkv0, ssm0, rwkv_w, gdn_w, ns, waves, nb, log_c):
    n_t = proj_a.shape[1] // SLOT_ROWS
    n_seq = ns * nb
    prev_rows = (SUBLANES if nb == 1 else SLOT_ROWS) * ns
    data = lambda g, t: (g, t, 0)
    grp3 = lambda g, t: (g, 0, 0)
    grp4 = lambda g, t: (g, 0, 0, 0)
    wkv_spec = pl.BlockSpec((n_seq, A_HEADS, A_HEAD, A_HEAD), grp4)
    shift_spec = pl.BlockSpec((n_seq, 1, A_SHIFT_W), grp3)
    ssm_spec = pl.BlockSpec((n_seq, B_HEADS, B_HEAD, B_HEAD), grp4)
    kern = functools.partial(_mixers_kernel, ns=ns, waves=waves, nb=nb, log_c=log_c, n_t=n_t)
    return pl.pallas_call(
        kern,
        grid=(proj_a.shape[0] // ns, n_t),
        in_specs=[pl.BlockSpec((ns, SLOT_ROWS, A_SHIFT_W), data),
                  pl.BlockSpec((ns, SLOT_ROWS, B_MIX_W), data),
                  pl.BlockSpec((ns, SLOT_ROWS, LANES), data),
                  wkv_spec, shift_spec, ssm_spec]
                 + [_const_spec(w.shape) for w in rwkv_w + gdn_w],
        out_specs=[pl.BlockSpec((ns, SLOT_ROWS, A_WIDTH), data),
                   pl.BlockSpec((ns, SLOT_ROWS, B_WIDTH), data),
                   wkv_spec, shift_spec, ssm_spec],
        out_shape=[jax.ShapeDtypeStruct(proj_a.shape[:2] + (A_WIDTH,), BF16),
                   jax.ShapeDtypeStruct(proj_b.shape[:2] + (B_WIDTH,), BF16),
                   jax.ShapeDtypeStruct(wkv0.shape, F32),
                   jax.ShapeDtypeStruct(shift0.shape, F32),
                   jax.ShapeDtypeStruct(ssm0.shape, F32)],
        scratch_shapes=[pltpu.VMEM((n_seq, A_HEADS // 2, LANES, LANES), F32),
                        pltpu.VMEM((prev_rows, A_SHIFT_W), F32),
                        pltpu.VMEM((n_seq, B_HEADS, B_HEAD, B_HEAD), F32)],
        compiler_params=pltpu.CompilerParams(
            dimension_semantics=("arbitrary", "arbitrary"), vmem_limit_bytes=VMEM_LIMIT_BYTES),
        name="token_mixers",
    )(proj_a, proj_b, proj_g, wkv0, shift0, ssm0, *rwkv_w, *gdn_w)


def _layer_norm(x, g, b):
    mu = jnp.mean(x, axis=-1, keepdims=True)
    d = x - mu
    var = jnp.mean(d * d, axis=-1, keepdims=True)
    return d * lax.rsqrt(var + LN_EPS) * g + b


def _post_kernel(oa_ref, ob_ref, x_ref, p_ref, fb_ref, woa_ref, wob_ref, lnv_ref, wgu_ref,
                 fv_ref, wd_ref, wple_ref, wpg_ref, y_ref, fout_ref, prev_scr, *, c):
    t_idx = pl.program_id(1)

    @pl.when(t_idx == 0)
    def _init():
        _load_prev(prev_scr, fb_ref, FFN_CONV - 1)

    ln1_g, ln1_b = lnv_ref[0:1, :], lnv_ref[1:2, :]
    ln2_g, ln2_b = lnv_ref[2:3, :], lnv_ref[3:4, :]
    ple_g = lnv_ref[4:5, :]

    mix = _dot(oa_ref[...].astype(BF16), woa_ref[...]) + _dot(ob_ref[...].astype(BF16), wob_ref[...])
    h = _layer_norm(DN_ALPHA * x_ref[...] + mix, ln1_g, ln1_b)
    hb = h.astype(BF16)
    ffn = None
    for f0, f1 in zip(FF_CUTS[:-1], FF_CUTS[1:]):
        fs = slice(f0, f1)
        gate = _dot(hb, wgu_ref[:, fs])
        up = _dot(hb, wgu_ref[:, D_FF + f0:D_FF + f1])
        prev = prev_scr[:, fs]
        gate_c = (fv_ref[0:1, fs] * _shift_rows(gate, 2, prev, c) + fv_ref[1:2, fs] * _shift_rows(gate, 1, prev, c)
                  + fv_ref[2:3, fs] * gate)
        _store_tails(gate, c, FFN_CONV - 1, prev_scr.at[:, fs], fout_ref.at[:, :, fs])
        act = _silu(gate_c + fv_ref[3:4, fs]) * up
        part = _dot(act.astype(BF16), wd_ref[fs, :])
        ffn = part if ffn is None else ffn + part
    h2 = _layer_norm(DN_ALPHA * h + ffn, ln2_g, ln2_b)
    pe = _dot(p_ref[...].astype(BF16), wple_ref[...])
    e = pe * lax.rsqrt(jnp.mean(pe * pe, axis=-1, keepdims=True) + RMS_EPS) * ple_g
    y_ref[...] = h2 + _sigmoid(_dot(h2.astype(BF16), wpg_ref[...])) * e


def _post_call(o_a, o_b, x2, p2, fconv0, wts, seq):
    rows = POST_ROWS[0] if seq % POST_ROWS[0] == 0 else POST_ROWS[1]
    c, nb, n_t, groups = _seq_blocking(rows, seq, fconv0.shape[0])
    prev_rows = SUBLANES if nb == 1 else rows
    row_map = lambda g, t: (g * n_t + t, 0)
    grp3 = lambda g, t: (g, 0, 0)
    kern = functools.partial(_post_kernel, c=c)
    return pl.pallas_call(
        kern,
        grid=(groups, n_t),
        in_specs=[pl.BlockSpec((rows, A_WIDTH), row_map),
                  pl.BlockSpec((rows, B_WIDTH), row_map),
                  pl.BlockSpec((rows, D_MODEL), row_map),
                  pl.BlockSpec((rows, PLE_DIM), row_map),
                  pl.BlockSpec((nb, FFN_CONV - 1, D_FF), grp3)]
                 + [_const_spec(w.shape) for w in wts],
        out_specs=[pl.BlockSpec((rows, D_MODEL), row_map),
                   pl.BlockSpec((nb, FFN_CONV - 1, D_FF), grp3)],
        out_shape=[jax.ShapeDtypeStruct(x2.shape, F32),
                   jax.ShapeDtypeStruct(fconv0.shape, F32)],
        scratch_shapes=[pltpu.VMEM((prev_rows, D_FF), F32)],
        compiler_params=pltpu.CompilerParams(
            dimension_semantics=("arbitrary", "arbitrary"), vmem_limit_bytes=VMEM_LIMIT_BYTES),
        name="post_mix_ffn",
    )(o_a, o_b, x2, p2, fconv0, *wts)


def _pad_rows(w, n):
    return jnp.pad(w, ((0, n - w.shape[0]), (0, 0)))


def _prepare_weights(w_in, a_mu, a_w0, a_w_w2, a_a0, a_w_a2, a_w_g2, a_k_k, a_k_a, a_r_k, a_gn_g,
                     a_gn_b, b_conv_w, b_a_log, b_dt_bias, b_norm_g, w_o, ln1_g, ln1_b, w_up,
                     f_conv_w, f_conv_b, w_down, ln2_g, ln2_b, w_ple, ple_g, w_ple_gate):
    lora = a_w_w2.shape[1]
    proj = (w_in[0].astype(BF16), _pad_rows(b_conv_w[0], SUBLANES))
    rwkv = (a_mu,
            _pad_rows(jnp.concatenate([a_w0, a_a0, a_k_k, a_k_a, a_r_k, a_gn_g, a_gn_b], axis=0), SUBLANES),
            _pad_rows(a_w_w2[0], LANES).astype(BF16),
            jnp.pad(a_w_a2[0], ((lora, LANES - 2 * lora), (0, 0))).astype(BF16),
            a_w_g2[0].astype(BF16))
    head_row = lambda vec: jnp.pad(vec, ((0, 0), (B_HEADS, LANES - 2 * B_HEADS)))
    gdn = (_pad_rows(jnp.concatenate([head_row(b_a_log), head_row(b_dt_bias), b_norm_g], axis=0), SUBLANES),)
    post = (w_o[0, :A_WIDTH].astype(BF16), w_o[0, A_WIDTH:].astype(BF16),
            _pad_rows(jnp.concatenate([ln1_g, ln1_b, ln2_g, ln2_b, ple_g], axis=0), SUBLANES),
            w_up[0].astype(BF16),
            _pad_rows(jnp.concatenate([f_conv_w[0], f_conv_b], axis=0), SUBLANES),
            w_down[0].astype(BF16), w_ple[0].astype(BF16), w_ple_gate[0].astype(BF16))
    return proj, rwkv, gdn, post


def _trunk(x, p, states, weights, proj):
    _, rwkv_w, gdn_w, post_w = weights
    a_wkv, a_shift, b_ssm, _, f_conv = (s[0] for s in states)
    bsz, seq, _ = x.shape
    m = bsz * seq
    x2 = x.reshape(m, D_MODEL)
    p2 = p.reshape(m, PLE_DIM)
    if seq % SLOT_ROWS == 0:
        nb, log_c, slots, kind = 1, SLOT_ROWS.bit_length() - 1, bsz, 0
    else:
        assert seq == SUBLANES
        nb, log_c, slots, kind = SLOT_ROWS // seq, 3, m // SLOT_ROWS, 1
    proj_a, proj_b, proj_g, conv_new = proj
    ns, waves = MIXER_SLOTS[kind]
    o_a, o_b, wkv_new, shift_new, ssm_new = _mixers_call(
        proj_a.reshape(slots, -1, A_SHIFT_W), proj_b.reshape(slots, -1, B_MIX_W), proj_g.reshape(slots, -1, LANES),
        a_shift[:, None, :], a_wkv, b_ssm, rwkv_w, gdn_w, ns, waves, nb, log_c)
    y2, fconv_new = _post_call(o_a.reshape(m, A_WIDTH), o_b.reshape(m, B_WIDTH), x2, p2, f_conv, post_w, seq)
    new_states = (wkv_new, shift_new[:, 0, :], ssm_new, conv_new, fconv_new)
    return y2.reshape(bsz, seq, D_MODEL), tuple(s[None] for s in new_states)


def kernel(x_prompt, x_sample, p_prompt, p_sample, state_a_wkv, state_a_shift, state_b_ssm, state_b_conv, state_ffn_conv, w_in, a_mu, a_w0, a_w_w2, a_a0, a_w_a2, a_w_g2, a_k_k, a_k_a, a_r_k, a_gn_g, a_gn_b, b_conv_w, b_a_log, b_dt_bias, b_norm_g, w_o, ln1_g, ln1_b, w_up, f_conv_w, f_conv_b, w_down, ln2_g, ln2_b, w_ple, ple_g, w_ple_gate):
    assert w_in.shape[0] == DEPTH
    weights = _prepare_weights(w_in, a_mu, a_w0, a_w_w2, a_a0, a_w_a2, a_w_g2, a_k_k, a_k_a, a_r_k,
                               a_gn_g, a_gn_b, b_conv_w, b_a_log, b_dt_bias, b_norm_g, w_o, ln1_g,
                               ln1_b, w_up, f_conv_w, f_conv_b, w_down, ln2_g, ln2_b, w_ple, ple_g,
                               w_ple_gate)
    bp = x_prompt.shape[0]
    zeros = lambda *s: jnp.zeros((DEPTH, bp) + s, x_prompt.dtype)
    prompt_init = (zeros(A_HEADS, A_HEAD, A_HEAD), zeros(A_SHIFT_W), zeros(B_HEADS, B_HEAD, B_HEAD),
                   zeros(B_CONV - 1, B_QKV), zeros(FFN_CONV - 1, D_FF))
    sample_init = (state_a_wkv, state_a_shift, state_b_ssm, state_b_conv, state_ffn_conv)
    rows = lambda x: x.reshape(-1, D_MODEL)
    prompt_proj, sample_proj = _proj_call(rows(x_prompt), prompt_init[3][0], x_prompt.shape[1],
                                          rows(x_sample), state_b_conv[0], x_sample.shape[1], *weights[0])
    y_prompt, prompt_states = _trunk(x_prompt, p_prompt[0], prompt_init, weights, prompt_proj)
    y_sample, sample_states = _trunk(x_sample, p_sample[0], sample_init, weights, sample_proj)
    return (y_prompt, y_sample) + prompt_states + sample_states
```

```python
import functools
import math

import jax
import jax.numpy as jnp
from jax import lax
from jax.experimental import pallas as pl
from jax.experimental.pallas import tpu as pltpu

F32 = jnp.float32
BF16 = jnp.bfloat16

LANES = 128
SUBLANES = 8
MXU_TILE = 256
VMEM_LIMIT_BYTES = 56 * 1024 * 1024

D_MODEL = 1024
A_WIDTH = 512
A_HEAD = 64
LOG_A_HEAD = 6
A_HEADS = 8
A_SHIFT_W = 1792
B_WIDTH = 512
B_HEAD = 128
LOG_B_HEAD = 7
B_HEADS = 4
B_CONV = 4
B_QKV = 1536
B_MIX_W = 2048
D_FF = 2816
FFN_CONV = 3
PLE_DIM = 256
DEPTH = 1
DN_ALPHA = (2.0 * DEPTH) ** 0.25
LN_EPS = 1e-5
GN_EPS = 64e-5
RMS_EPS = 1e-6
L2_EPS = 1e-12

SLOT_ROWS = 64
POST_ROWS = (512, 256)
PROJ_ROWS = 512
FF_CUTS = (0, 1536, 2816)
MIXER_SLOTS = ((8, (4, 8)), (2, (2, 2)))


def _dot(a, b):
    return jnp.dot(a, b, preferred_element_type=F32)


def _mm(a, b):
    return _dot(a.astype(BF16), b.astype(BF16))


def _split(x, n):
    parts = []
    r = x
    for i in range(n):
        h = r.astype(BF16)
        parts.append(h)
        if i + 1 < n:
            r = r - h.astype(F32)
    return parts


def _mm_nt(a, b):
    return lax.dot_general(a.astype(BF16), b.astype(BF16), (((1,), (1,)), ((), ())), preferred_element_type=F32)


def _mm_exact_lhs(m, x, n):
    out = None
    for part in _split(x, n):
        d = _dot(m, part)
        out = d if out is None else out + d
    return out


def _mm_exact_rhs(x, m, n):
    out = None
    for part in _split(x, n):
        d = _dot(part, m)
        out = d if out is None else out + d
    return out


def _segment_sums(x, log_seg, n_split):
    tile = min(MXU_TILE, x.shape[1])
    same = (jnp.right_shift(_iota((tile, tile), 0), log_seg) == jnp.right_shift(_iota((tile, tile), 1), log_seg))
    ones = jnp.where(same, 1.0, 0.0).astype(BF16)
    parts = [_mm_exact_rhs(x[:, s:s + tile], ones, n_split) for s in range(0, x.shape[1], tile)]
    return parts[0] if len(parts) == 1 else jnp.concatenate(parts, axis=1)


def _sigmoid(x):
    return 1.0 / (1.0 + jnp.exp2(x * (-math.log2(math.e))))


def _silu(x):
    return x * _sigmoid(x)


def _softplus(x):
    return jnp.maximum(x, 0.0) + jnp.log1p(jnp.exp(-jnp.abs(x)))


def _iota(shape, dim):
    return lax.broadcasted_iota(jnp.int32, shape, dim)


def _shift_rows(x, j, prev, c):
    rows = x.shape[0]
    rolled = pltpu.roll(x, j, 0)
    if c == SUBLANES:
        t = jnp.bitwise_and(_iota((rows, 1), 0), SUBLANES - 1)
        return jnp.where(t >= j, rolled, pltpu.roll(prev, rows - SUBLANES + j, 0))
    t = _iota((SUBLANES, 1), 0)
    parts = []
    for i in range(rows // c):
        head = jnp.where(t >= j, rolled[i * c:i * c + SUBLANES],
                         pltpu.roll(prev[i * SUBLANES:(i + 1) * SUBLANES], j, 0))
        parts += [head, rolled[i * c + SUBLANES:(i + 1) * c]]
    return jnp.concatenate(parts, axis=0)


def _load_prev(prev_scr, buf_ref, k_prev):
    prev_scr[...] = jnp.zeros(prev_scr.shape, F32)
    for i in range(buf_ref.shape[0]):
        prev_scr[SUBLANES * (i + 1) - k_prev:SUBLANES * (i + 1), :] = buf_ref[i]


def _store_tails(x, c, k_prev, prev_scr, tail_ref, seq0=0):
    for i in range(x.shape[0] // c):
        q = seq0 + i
        tail_ref[q] = x[(i + 1) * c - k_prev:(i + 1) * c, :]
        if c > SUBLANES:
            prev_scr[q * SUBLANES:(q + 1) * SUBLANES, :] = x[(i + 1) * c - SUBLANES:(i + 1) * c, :]


def _waves(n_waves, pre, work):
    ready = yield from pre(0)
    for w in range(n_waves):
        cur = work(w, ready)
        nxt = pre(w + 1) if w + 1 < n_waves else None
        cur_done = False
        while not cur_done or nxt is not None:
            if not cur_done:
                try:
                    next(cur)
                except StopIteration:
                    cur_done = True
            if nxt is not None:
                try:
                    next(nxt)
                except StopIteration as stop:
                    ready = stop.value
                    nxt = None
            yield


def _alternate(*gens):
    gens = list(gens)
    while gens:
        for gen in list(gens):
            try:
                next(gen)
            except StopIteration:
                gens.remove(gen)


def _seq_cumsum_matrix(rows, log_c):
    r = _iota((rows, rows), 0)
    q = _iota((rows, rows), 1)
    same = jnp.right_shift(r, log_c) == jnp.right_shift(q, log_c)
    return jnp.where(jnp.logical_and(same, r >= q), 1.0, 0.0).astype(BF16)


def _folded_masks(rows, log_c):
    r = _iota((rows, 2 * rows), 0)
    q = jnp.bitwise_and(_iota((rows, 2 * rows), 1), rows - 1)
    same = jnp.right_shift(r, log_c) == jnp.right_shift(q, log_c)
    return jnp.logical_and(same, r > q), jnp.logical_and(same, r >= q)


def _proj_kernel(xl_ref, xs_ref, w_ref, cbl_ref, cbs_ref, cw_ref,
                 pal_ref, pbl_ref, pgl_ref, coutl_ref, pas_ref, pbs_ref, pgs_ref, couts_ref,
                 prevl_scr, prevs_scr, *, long_steps, steps_per_seq, c_long, c_short):
    i = pl.program_id(0)

    @pl.when(i < long_steps)
    def _long():
        _proj_block(xl_ref, w_ref, cbl_ref, cw_ref, pal_ref, pbl_ref, pgl_ref, coutl_ref, prevl_scr,
                    c=c_long, first=lax.rem(i, steps_per_seq) == 0)

    @pl.when(i >= long_steps)
    def _short():
        _proj_block(xs_ref, w_ref, cbs_ref, cw_ref, pas_ref, pbs_ref, pgs_ref, couts_ref, prevs_scr,
                    c=c_short, first=i >= long_steps)


def _proj_block(x_ref, w_ref, cb_ref, cw_ref, pa_ref, pb_ref, pg_ref, cout_ref, prev_scr, *, c, first):
    @pl.when(first)
    def _init():
        _load_prev(prev_scr, cb_ref, B_CONV - 1)

    xb = x_ref[...].astype(BF16)
    in_w = w_ref.shape[1]
    gates_w = in_w - A_SHIFT_W - B_QKV - B_WIDTH
    pb = _dot(xb, w_ref[:, A_SHIFT_W:in_w])
    x = pb[:, 0:B_QKV]
    prev = prev_scr[...]
    conv = (cw_ref[0:1, :] * _shift_rows(x, 3, prev, c) + cw_ref[1:2, :] * _shift_rows(x, 2, prev, c)
            + cw_ref[2:3, :] * _shift_rows(x, 1, prev, c) + cw_ref[3:4, :] * x)
    _store_tails(x, c, B_CONV - 1, prev_scr, cout_ref)
    qkv = _silu(conv)
    qk = qkv[:, 0:2 * B_WIDTH]
    qk = qk * lax.rsqrt(_segment_sums(qk * qk, LOG_B_HEAD, 1) + L2_EPS)
    pb_ref[:, 0:B_WIDTH] = (qk[:, 0:B_WIDTH] * (B_HEAD ** -0.5)).astype(BF16)
    pb_ref[:, B_WIDTH:2 * B_WIDTH] = qk[:, B_WIDTH:2 * B_WIDTH].astype(BF16)
    pb_ref[:, 2 * B_WIDTH:B_QKV] = qkv[:, 2 * B_WIDTH:B_QKV].astype(BF16)
    pb_ref[:, B_QKV:B_QKV + B_WIDTH] = _silu(pb[:, B_QKV:B_QKV + B_WIDTH]).astype(BF16)
    pg_ref[:, 0:gates_w] = pb[:, B_QKV + B_WIDTH:B_QKV + B_WIDTH + gates_w]
    pg_ref[:, gates_w:LANES] = jnp.zeros((pg_ref.shape[0], LANES - gates_w), F32)
    pa_ref[...] = _dot(xb, w_ref[:, 0:A_SHIFT_W]).astype(BF16)


def _const_spec(shape):
    zeros = (0,) * len(shape)
    return pl.BlockSpec(shape, lambda *_: zeros, pipeline_mode=pl.Buffered(1))


def _seq_blocking(rows, seq, n_seq):
    c = rows if seq % rows == 0 else seq
    nb = rows // c
    return c, nb, seq // c, n_seq // nb


def _proj_call(x_long, conv_long, seq_long, x_short, conv_short, seq_short, w_in, conv_w):
    rows = PROJ_ROWS
    c_long, nb_long, n_t, groups = _seq_blocking(rows, seq_long, conv_long.shape[0])
    c_short, nb_short, short_t, short_steps = _seq_blocking(rows, seq_short, conv_short.shape[0])
    assert nb_long == 1 and short_t == 1
    long_steps = groups * n_t
    long_blk = lambda i: jnp.minimum(i, long_steps - 1)
    short_blk = lambda i: jnp.maximum(i - long_steps, 0)

    def specs(blk, nb, per_seq):
        row_map = lambda i: (blk(i), 0)
        return ([pl.BlockSpec((rows, w), row_map) for w in (D_MODEL, A_SHIFT_W, B_MIX_W, LANES)],
                pl.BlockSpec((nb, B_CONV - 1, B_QKV), lambda i: (blk(i) // per_seq, 0, 0)))

    def shapes(m, conv0):
        return [jax.ShapeDtypeStruct((m, A_SHIFT_W), BF16), jax.ShapeDtypeStruct((m, B_MIX_W), BF16),
                jax.ShapeDtypeStruct((m, LANES), F32), jax.ShapeDtypeStruct(conv0.shape, F32)]

    (xl_spec, *outl_specs), cl_spec = specs(long_blk, nb_long, n_t)
    (xs_spec, *outs_specs), cs_spec = specs(short_blk, nb_short, 1)
    outs = pl.pallas_call(
        functools.partial(_proj_kernel, long_steps=long_steps, steps_per_seq=n_t, c_long=c_long, c_short=c_short),
        grid=(long_steps + short_steps,),
        in_specs=[xl_spec, xs_spec, _const_spec(w_in.shape), cl_spec, cs_spec, _const_spec(conv_w.shape)],
        out_specs=outl_specs + [cl_spec] + outs_specs + [cs_spec],
        out_shape=shapes(x_long.shape[0], conv_long) + shapes(x_short.shape[0], conv_short),
        scratch_shapes=[pltpu.VMEM((SUBLANES, B_QKV), F32), pltpu.VMEM((rows, B_QKV), F32)],
        compiler_params=pltpu.CompilerParams(
            dimension_semantics=("arbitrary",), vmem_limit_bytes=VMEM_LIMIT_BYTES),
        name="in_proj",
    )(x_long, x_short, w_in, conv_long, conv_short, conv_w)
    return outs[0:4], outs[4:8]


def _rwkv_parts(pa_ref, s0_ref, sh0_ref, mu_ref, vec_ref, ww2_ref, wa2_ref, wg2_ref,
                o_ref, sfin_ref, shout_ref, s_scr, prev_scr, *, ns, wave, nb, log_c, n_t):
    c = 1 << log_c
    rows = SLOT_ROWS
    n2 = 2 * rows
    pairs = A_HEADS // 2
    n_seq = ns * nb
    t_idx = pl.program_id(1)
    key_major = n_t > 1

    @pl.when(t_idx == 0)
    def _init():
        z = jnp.zeros((A_HEAD, A_HEAD), F32)
        for i in range(n_seq):
            for p in range(pairs):
                top = jnp.concatenate([s0_ref[i, 2 * p], z], axis=1)
                bot = jnp.concatenate([z, s0_ref[i, 2 * p + 1]], axis=1)
                s_pair = jnp.concatenate([top, bot], axis=0)
                s_scr[i, p] = s_pair.T if key_major else s_pair
        _load_prev(prev_scr, sh0_ref, 1)

    w0 = vec_ref[0:1, :]
    a0 = vec_ref[1:2, :]
    k_k = vec_ref[2:3, :]
    k_a = vec_ref[3:4, :]
    r_k = vec_ref[4:5, :]
    gn_g = vec_ref[5:6, :]
    gn_b = vec_ref[6:7, :]

    def head_sum(x, n_split):
        return _segment_sums(x, LOG_A_HEAD, n_split)

    strict, incl = _folded_masks(rows, log_c)
    lane0 = _iota((1, LANES), 1) < A_HEAD
    col_seq = jnp.bitwise_and(jnp.right_shift(_iota((1, n2), 1), log_c), nb - 1)

    def stack(xp):
        return jnp.concatenate([jnp.where(lane0, xp, 0.0), jnp.where(lane0, 0.0, xp)], axis=0)

    def seq_cols(xt, b):
        return xt if nb == 1 else jnp.where(col_seq == b, xt, 0.0)

    def pre(w):
        slot0 = w * wave
        seq0 = slot0 * nb
        u = pa_ref[slot0:slot0 + wave].reshape(wave * rows, A_SHIFT_W).astype(F32)
        prev = prev_scr[seq0 * SUBLANES:(seq0 + wave * nb) * SUBLANES, :]
        xs = u + (_shift_rows(u, 1, prev, c) - u) * mu_ref[...]
        _store_tails(u, c, 1, prev_scr, shout_ref, seq0)
        r = xs[:, 0:A_WIDTH]
        k = xs[:, A_WIDTH:2 * A_WIDTH]
        v = xs[:, 2 * A_WIDTH:3 * A_WIDTH]
        wa = xs[:, 3 * A_WIDTH:3 * A_WIDTH + LANES]
        gd = xs[:, 3 * A_WIDTH + LANES:A_SHIFT_W]
        wlora = _dot(jnp.tanh(wa).astype(BF16), ww2_ref[...])
        alora = _dot(wa.astype(BF16), wa2_ref[...])
        g = _dot(_sigmoid(gd).astype(BF16), wg2_ref[...])
        kkr = k * k_k
        kk_ss = head_sum(kkr * kkr, 1)
        yield
        lw = -math.exp(-0.5) * _sigmoid(w0 + wlora)
        gcum = _mm_exact_lhs(_seq_cumsum_matrix(wave * rows, log_c), lw, 2)
        a = _sigmoid(a0 + alora)
        kk = kkr * lax.rsqrt(kk_ss + L2_EPS)
        k2 = k * (1.0 + (a - 1.0) * k_a)
        bonus = head_sum(r * k2 * r_k, 1) * v
        yield
        e_g = jnp.exp(gcum)
        e_gi = jnp.exp(-gcum)
        a_t = -kk * jnp.exp(gcum - lw)
        b_t = kk * a * e_gi
        k_t = k2 * e_gi
        r_t = r * e_g
        return a_t, b_t, k_t, r_t, v, e_g, bonus, g

    def work(w, prepared):
        a_t, b_t, k_t, r_t, v, e_g, bonus, g = prepared
        slot0 = w * wave
        chains = [(s, p) for s in range(wave) for p in range(pairs)]
        a_l, r_l, v_l, vt_l, b_l, k_l, n_l, ak_l, rb_l, rk_l, dec_l = ([] for _ in range(11))
        for s, p in chains:
            rs = slice(s * rows, (s + 1) * rows)
            sl = slice(p * LANES, (p + 1) * LANES)
            bk_s = jnp.concatenate([stack(b_t[rs, sl]), stack(k_t[rs, sl])], axis=0)
            bk_t = bk_s.T
            x = _mm(jnp.concatenate([a_t[rs, sl], r_t[rs, sl]], axis=0), bk_t)
            v_s = stack(v[rs, sl])
            a_l.append(a_t[rs, sl])
            r_l.append(r_t[rs, sl])
            v_l.append(v_s.astype(BF16))
            if key_major:
                b_l.append(bk_t[:, 0:n2])
                k_l.append(bk_t[:, n2:2 * n2])
            else:
                b_l.append(bk_s[0:n2].astype(BF16))
                k_l.append(bk_s[n2:2 * n2].astype(BF16))
                vt_l.append(v_s.T)
            n_l.append(jnp.where(strict, x[0:rows, 0:n2], 0.0))
            ak_l.append(jnp.where(strict, x[0:rows, n2:2 * n2], 0.0))
            rb_l.append(jnp.where(incl, x[rows:n2, 0:n2], 0.0))
            rk_l.append(jnp.where(incl, x[rows:n2, n2:2 * n2], 0.0))
            dec_l.append(e_g[rs, sl].T if key_major else e_g[rs, sl])
        yield
        tm_l = list(n_l)
        pw_l = list(n_l)
        pwb_l = [stack(q).astype(BF16) for q in pw_l]
        for _ in range(log_c - 1):
            pw_l = [_mm(q, qb) for q, qb in zip(pw_l, pwb_l)]
            pwb_l = [stack(q).astype(BF16) for q in pw_l]
            tm_l = [t + q + _mm(t, qb) for t, q, qb in zip(tm_l, pw_l, pwb_l)]
            yield
        akv_l, rkv_l, kv_l = [], [], []
        for ci in range(len(chains)):
            lhs = [ak_l[ci], rk_l[ci]] + ([seq_cols(k_l[ci], b) for b in range(nb)] if key_major else [])
            xv = _mm(jnp.concatenate(lhs, axis=0), v_l[ci])
            akv_l.append(xv[0:rows])
            rkv_l.append(xv[rows:n2])
            if key_major:
                kv = xv[n2:(nb + 1) * n2]
            else:
                kv = _mm(jnp.concatenate([seq_cols(vt_l[ci], b) for b in range(nb)], axis=0), k_l[ci])
            kv_l.append([kv[b * n2:(b + 1) * n2] for b in range(nb)])
        yield
        wu_l = []
        for ci in range(len(chains)):
            rhs = jnp.concatenate([a_l[ci], akv_l[ci]], axis=1)
            rhs_s = jnp.concatenate([stack(a_l[ci]), stack(akv_l[ci])], axis=1)
            wu_l.append(rhs + _mm(tm_l[ci], rhs_s))
        yield
        sa_l, rs_l = [], []
        for ci, (s, p) in enumerate(chains):
            sa_parts, rs_parts = [], []
            for b in range(nb):
                m_old = s_scr[(slot0 + s) * nb + b, p]
                wr = jnp.concatenate([wu_l[ci][b * c:(b + 1) * c, 0:LANES], r_l[ci][b * c:(b + 1) * c]], axis=0)
                xm = _mm(wr, m_old) if key_major else _mm_nt(wr, m_old)
                sa_parts.append(xm[0:c] + wu_l[ci][b * c:(b + 1) * c, LANES:2 * LANES])
                rs_parts.append(xm[c:2 * c])
            sa_l.append(sa_parts[0] if nb == 1 else jnp.concatenate(sa_parts, axis=0))
            rs_l.append(rs_parts[0] if nb == 1 else jnp.concatenate(rs_parts, axis=0))
        yield
        o_l = []
        for ci, (s, p) in enumerate(chains):
            sas = stack(sa_l[ci])
            if key_major:
                lhs = [seq_cols(b_l[ci], b) for b in range(nb)] + [rb_l[ci]]
                xs_ = _mm(jnp.concatenate(lhs, axis=0), sas)
                rb_sa = xs_[nb * n2:nb * n2 + rows]
            else:
                sas_t = sas.T
                xs_ = _mm(jnp.concatenate([seq_cols(sas_t, b) for b in range(nb)], axis=0), b_l[ci])
                rb_sa = _mm(rb_l[ci], sas)
            for b in range(nb):
                q = (slot0 + s) * nb + b
                upd = xs_[b * n2:(b + 1) * n2] + kv_l[ci][b]
                last = (b + 1) * c - 1
                decay = dec_l[ci][:, last:last + 1] if key_major else dec_l[ci][last:last + 1, :]
                s_scr[q, p] = (s_scr[q, p] + upd) * decay
            o_l.append(rs_l[ci] + rb_sa + rkv_l[ci])
        o_rows = [jnp.concatenate(o_l[s * pairs:(s + 1) * pairs], axis=1) for s in range(wave)]
        o = jnp.concatenate(o_rows, axis=0)
        yield
        mean = head_sum(o, 2) * (1.0 / A_HEAD)
        d = o - mean
        var = head_sum(d * d, 1) * (1.0 / A_HEAD)
        out = (d * lax.rsqrt(var + GN_EPS) * gn_g + gn_b + bonus) * g
        o_ref[slot0:slot0 + wave] = out.reshape(wave, rows, A_WIDTH).astype(o_ref.dtype)

    def finish():
        @pl.when(t_idx == n_t - 1)
        def _fin():
            for i in range(n_seq):
                for p in range(pairs):
                    s_pair = s_scr[i, p].T if key_major else s_scr[i, p]
                    sfin_ref[i, 2 * p] = s_pair[0:A_HEAD, 0:A_HEAD]
                    sfin_ref[i, 2 * p + 1] = s_pair[A_HEAD:LANES, A_HEAD:LANES]

    return ns // wave, pre, work, finish


def _gdn_parts(pb_ref, pg_ref, s0_ref, hv_ref, o_ref, sfin_ref, s_scr, *, ns, wave, nb, log_c, n_t):
    c = 1 << log_c
    rows = SLOT_ROWS
    n2 = 2 * rows
    units = 2 * nb
    npairs = B_HEADS // 2
    t_idx = pl.program_id(1)

    @pl.when(t_idx == 0)
    def _init():
        s_scr[...] = s0_ref[...]

    a_log = hv_ref[0:1, :]
    dt_bias = hv_ref[1:2, :]
    norm_g = jnp.concatenate([hv_ref[2:3, :]] * B_HEADS, axis=1)

    strict, incl = _folded_masks(rows, log_c)
    lane0 = _iota((1, LANES), 1) < rows

    def stack(xf):
        return jnp.concatenate([jnp.where(lane0, xf, 0.0), jnp.where(lane0, 0.0, xf)], axis=0)

    row2 = _iota((n2, n2), 0)
    col2 = _iota((n2, n2), 1)
    last_sel_t = row2 == jnp.bitwise_or(col2, c - 1)
    col_unit = jnp.right_shift(_iota((1, n2), 1), log_c)

    def pre(w):
        slot0 = w * wave
        pb = pb_ref[slot0:slot0 + wave].reshape(wave * rows, B_MIX_W)
        ba = pg_ref[slot0:slot0 + wave].reshape(wave * rows, LANES)
        beta_t = _sigmoid(ba)
        g_t = -jnp.exp(a_log) * _softplus(ba + dt_bias)
        gcum_t = _mm_exact_lhs(_seq_cumsum_matrix(wave * rows, log_c), g_t, 3)
        yield
        return (pb[:, 0:B_WIDTH], pb[:, B_WIDTH:2 * B_WIDTH], pb[:, 2 * B_WIDTH:B_QKV], beta_t, gcum_t,
                pb[:, B_QKV:B_MIX_W].astype(F32))

    def state_index(slot, p, unit):
        return slot * nb + unit % nb, 2 * p + unit // nb

    def work(w, prepared):
        qn, kn, vv, beta_t, gcum_t, zs = prepared
        slot0 = w * wave
        chains = [(s, p) for s in range(wave) for p in range(npairs)]
        nl_l, rhs_l, attn_l, qdec_l, kdt_l, gl_l = ([] for _ in range(6))
        for s, p in chains:
            rs = slice(s * rows, (s + 1) * rows)
            hs = (2 * p, 2 * p + 1)
            q_s = jnp.concatenate([qn[rs, h * LANES:(h + 1) * LANES] for h in hs], axis=0)
            k_s = jnp.concatenate([kn[rs, h * LANES:(h + 1) * LANES] for h in hs], axis=0)
            v_s = jnp.concatenate([vv[rs, h * LANES:(h + 1) * LANES] for h in hs], axis=0)
            beta_s = jnp.broadcast_to(jnp.concatenate([beta_t[rs, h:h + 1] for h in hs], axis=0), (n2, LANES))
            g_s = jnp.concatenate([gcum_t[rs, B_HEADS + h:B_HEADS + h + 1] for h in hs], axis=0)
            gi = jnp.broadcast_to(g_s, (n2, n2))
            gj = gi.T
            gi_f = jnp.where(lane0, gi[0:rows], gi[rows:n2])
            dec_f = jnp.where(incl, jnp.exp(gi_f - gj[0:rows]), 0.0)
            kb = k_s * beta_s
            ks_t = k_s.T
            x = _mm(jnp.concatenate([kb, q_s], axis=0), ks_t)
            e_gs = jnp.exp(gi)
            g_last = jnp.sum(jnp.where(last_sel_t, gi, 0.0), axis=0, keepdims=True)
            l_f = jnp.where(lane0, x[0:rows], x[rows:n2])
            nl_l.append(-(l_f * jnp.where(strict, dec_f, 0.0)))
            attn_l.append((x[n2:2 * n2] * stack(dec_f)).astype(BF16))
            rhs_l.append(jnp.concatenate([v_s * beta_s, kb * e_gs], axis=1))
            qdec_l.append(q_s * e_gs)
            kdt_l.append(ks_t * jnp.exp(g_last - gj[0:1, :]))
            gl_l.append(jnp.exp(g_last))
        yield
        tm_l = list(nl_l)
        pw_l = list(nl_l)
        pwb_l = [stack(q).astype(BF16) for q in pw_l]
        for _ in range(log_c - 1):
            pw_l = [_mm(q, qb) for q, qb in zip(pw_l, pwb_l)]
            pwb_l = [stack(q).astype(BF16) for q in pw_l]
            tm_l = [t + q + _mm(t, qb) for t, q, qb in zip(tm_l, pw_l, pwb_l)]
            yield
        uw_l = [rhs + _mm(stack(tm), rhs) for tm, rhs in zip(tm_l, rhs_l)]
        yield
        vn_l, qs_l = [], []
        for ci, (s, p) in enumerate(chains):
            vn_parts, qs_parts = [], []
            for unit in range(units):
                s_old = s_scr[state_index(slot0 + s, p, unit)]
                us = slice(unit * c, (unit + 1) * c)
                vn_parts.append(uw_l[ci][us, 0:LANES] - _mm(uw_l[ci][us, LANES:2 * LANES], s_old))
                qs_parts.append(_mm(qdec_l[ci][us], s_old))
            vn_l.append(jnp.concatenate(vn_parts, axis=0).astype(BF16))
            qs_l.append(jnp.concatenate(qs_parts, axis=0))
        yield
        for ci, (s, p) in enumerate(chains):
            for unit in range(units):
                kd_u = jnp.where(col_unit == unit, kdt_l[ci], 0.0)
                idx = state_index(slot0 + s, p, unit)
                s_scr[idx] = s_scr[idx] * gl_l[ci][:, unit * c:unit * c + 1] + _mm(kd_u, vn_l[ci])
        o_rows = []
        for s in range(wave):
            heads = []
            for p in range(npairs):
                ci = s * npairs + p
                o_s = qs_l[ci] + _mm(attn_l[ci], vn_l[ci])
                heads += [o_s[0:rows], o_s[rows:n2]]
            o_rows.append(jnp.concatenate(heads, axis=1))
        o = jnp.concatenate(o_rows, axis=0)
        yield
        mean_sq = _segment_sums(o * o, LOG_B_HEAD, 1) * (1.0 / B_HEAD)
        out = o * lax.rsqrt(mean_sq + RMS_EPS) * norm_g * zs
        o_ref[slot0:slot0 + wave] = out.reshape(wave, rows, B_WIDTH).astype(o_ref.dtype)

    def finish():
        @pl.when(t_idx == n_t - 1)
        def _fin():
            sfin_ref[...] = s_scr[...]

    return ns // wave, pre, work, finish


def _mixers_kernel(pa_ref, pb_ref, pg_ref, wkv0_ref, sh0_ref, ssm0_ref, mu_ref, vec_ref, ww2_ref, wa2_ref,
                   wg2_ref, hv_ref, oa_ref, ob_ref, wkv_ref, shout_ref, ssm_ref, wkv_scr, prev_scr, ssm_scr,
                   *, ns, waves, nb, log_c, n_t):
    common = dict(ns=ns, nb=nb, log_c=log_c, n_t=n_t)
    a_waves, a_pre, a_work, a_finish = _rwkv_parts(
        pa_ref, wkv0_ref, sh0_ref, mu_ref, vec_ref, ww2_ref, wa2_ref, wg2_ref, oa_ref, wkv_ref, shout_ref,
        wkv_scr, prev_scr, wave=waves[0], **common)
    b_waves, b_pre, b_work, b_finish = _gdn_parts(
        pb_ref, pg_ref, ssm0_ref, hv_ref, ob_ref, ssm_ref, ssm_scr, wave=waves[1], **common)
    _alternate(_waves(a_waves, a_pre, a_work), _waves(b_waves, b_pre, b_work))
    a_finish()
    b_finish()


def _mixers_call(proj_a, proj_b, proj_g, shift0, wkv0, ssm0, rwkv_w, gdn_w, ns, waves, nb, log_c):
    n_t = proj_a.shape[1] // SLOT_ROWS
    n_seq = ns * nb
    prev_rows = (SUBLANES if nb == 1 else SLOT_ROWS) * ns
    data = lambda g, t: (g, t, 0)
    grp3 = lambda g, t: (g, 0, 0)
    grp4 = lambda g, t: (g, 0, 0, 0)
    wkv_spec = pl.BlockSpec((n_seq, A_HEADS, A_HEAD, A_HEAD), grp4)
    shift_spec = pl.BlockSpec((n_seq, 1, A_SHIFT_W), grp3)
    ssm_spec = pl.BlockSpec((n_seq, B_HEADS, B_HEAD, B_HEAD), grp4)
    kern = functools.partial(_mixers_kernel, ns=ns, waves=waves, nb=nb, log_c=log_c, n_t=n_t)
    return pl.pallas_call(
        kern,
        grid=(proj_a.shape[0] // ns, n_t),
        in_specs=[pl.BlockSpec((ns, SLOT_ROWS, A_SHIFT_W), data),
                  pl.BlockSpec((ns, SLOT_ROWS, B_MIX_W), data),
                  pl.BlockSpec((ns, SLOT_ROWS, LANES), data),
                  wkv_spec, shift_spec, ssm_spec]
                 + [_const_spec(w.shape) for w in rwkv_w + gdn_w],
        out_specs=[pl.BlockSpec((ns, SLOT_ROWS, A_WIDTH), data),
                   pl.BlockSpec((ns, SLOT_ROWS, B_WIDTH), data),
                   wkv_spec, shift_spec, ssm_spec],
        out_shape=[jax.ShapeDtypeStruct(proj_a.shape[:2] + (A_WIDTH,), BF16),
                   jax.ShapeDtypeStruct(proj_b.shape[:2] + (B_WIDTH,), BF16),
                   jax.ShapeDtypeStruct(wkv0.shape, F32),
                   jax.ShapeDtypeStruct(shift0.shape, F32),
                   jax.ShapeDtypeStruct(ssm0.shape, F32)],
        scratch_shapes=[pltpu.VMEM((n_seq, A_HEADS // 2, LANES, LANES), F32),
                        pltpu.VMEM((prev_rows, A_SHIFT_W), F32),
                        pltpu.VMEM((n_seq, B_HEADS, B_HEAD, B_HEAD), F32)],
        compiler_params=pltpu.CompilerParams(
            dimension_semantics=("arbitrary", "arbitrary"), vmem_limit_bytes=VMEM_LIMIT_BYTES),
        name="token_mixers",
    )(proj_a, proj_b, proj_g, wkv0, shift0, ssm0, *rwkv_w, *gdn_w)


def _layer_norm(x, g, b):
    mu = jnp.mean(x, axis=-1, keepdims=True)
    d = x - mu
    var = jnp.mean(d * d, axis=-1, keepdims=True)
    return d * lax.rsqrt(var + LN_EPS) * g + b


def _post_kernel(*refs, long_steps, steps_per_seq, c_long, c_short):
    in_long, in_short, wts = refs[0:5], refs[5:10], refs[10:18]
    out_long, out_short, (prev_long, prev_short) = refs[18:20], refs[20:22], refs[22:24]
    i = pl.program_id(0)

    @pl.when(i < long_steps)
    def _long():
        _post_block(*in_long, *wts, *out_long, prev_long, c=c_long, first=lax.rem(i, steps_per_seq) == 0)

    @pl.when(i >= long_steps)
    def _short():
        _post_block(*in_short, *wts, *out_short, prev_short, c=c_short, first=i >= long_steps)


def _post_block(oa_ref, ob_ref, x_ref, p_ref, fb_ref, woa_ref, wob_ref, lnv_ref, wgu_ref,
                fv_ref, wd_ref, wple_ref, wpg_ref, y_ref, fout_ref, prev_scr, *, c, first):
    @pl.when(first)
    def _init():
        _load_prev(prev_scr, fb_ref, FFN_CONV - 1)

    ln1_g, ln1_b = lnv_ref[0:1, :], lnv_ref[1:2, :]
    ln2_g, ln2_b = lnv_ref[2:3, :], lnv_ref[3:4, :]
    ple_g = lnv_ref[4:5, :]

    mix = _dot(oa_ref[...].astype(BF16), woa_ref[...]) + _dot(ob_ref[...].astype(BF16), wob_ref[...])
    h = _layer_norm(DN_ALPHA * x_ref[...] + mix, ln1_g, ln1_b)
    hb = h.astype(BF16)
    ffn = None
    for f0, f1 in zip(FF_CUTS[:-1], FF_CUTS[1:]):
        fs = slice(f0, f1)
        gate = _dot(hb, wgu_ref[:, fs])
        up = _dot(hb, wgu_ref[:, D_FF + f0:D_FF + f1])
        prev = prev_scr[:, fs]
        gate_c = (fv_ref[0:1, fs] * _shift_rows(gate, 2, prev, c) + fv_ref[1:2, fs] * _shift_rows(gate, 1, prev, c)
                  + fv_ref[2:3, fs] * gate)
        _store_tails(gate, c, FFN_CONV - 1, prev_scr.at[:, fs], fout_ref.at[:, :, fs])
        act = _silu(gate_c + fv_ref[3:4, fs]) * up
        part = _dot(act.astype(BF16), wd_ref[fs, :])
        ffn = part if ffn is None else ffn + part
    h2 = _layer_norm(DN_ALPHA * h + ffn, ln2_g, ln2_b)
    pe = _dot(p_ref[...].astype(BF16), wple_ref[...])
    e = pe * lax.rsqrt(jnp.mean(pe * pe, axis=-1, keepdims=True) + RMS_EPS) * ple_g
    y_ref[...] = h2 + _sigmoid(_dot(h2.astype(BF16), wpg_ref[...])) * e


def _post_call(ins_long, seq_long, ins_short, seq_short, wts):
    rows_long, rows_short = POST_ROWS
    c_long, nb_long, n_t, groups = _seq_blocking(rows_long, seq_long, ins_long[4].shape[0])
    c_short, nb_short, short_t, short_steps = _seq_blocking(rows_short, seq_short, ins_short[4].shape[0])
    assert nb_long == 1 and short_t == 1
    long_steps = groups * n_t
    long_blk = lambda i: jnp.minimum(i, long_steps - 1)
    short_blk = lambda i: jnp.maximum(i - long_steps, 0)

    def specs(blk, rows, nb, per_seq):
        row_map = lambda i: (blk(i), 0)
        tails = pl.BlockSpec((nb, FFN_CONV - 1, D_FF), lambda i: (blk(i) // per_seq, 0, 0))
        return [pl.BlockSpec((rows, w), row_map) for w in (A_WIDTH, B_WIDTH, D_MODEL, PLE_DIM)] + [tails]

    def shapes(ins):
        return [jax.ShapeDtypeStruct(ins[2].shape, F32), jax.ShapeDtypeStruct(ins[4].shape, F32)]

    specs_long = specs(long_blk, rows_long, nb_long, n_t)
    specs_short = specs(short_blk, rows_short, nb_short, 1)
    outs = pl.pallas_call(
        functools.partial(_post_kernel, long_steps=long_steps, steps_per_seq=n_t, c_long=c_long, c_short=c_short),
        grid=(long_steps + short_steps,),
        in_specs=specs_long + specs_short + [_const_spec(w.shape) for w in wts],
        out_specs=[specs_long[2], specs_long[4], specs_short[2], specs_short[4]],
        out_shape=shapes(ins_long) + shapes(ins_short),
        scratch_shapes=[pltpu.VMEM((SUBLANES, D_FF), F32), pltpu.VMEM((rows_short, D_FF), F32)],
        compiler_params=pltpu.CompilerParams(
            dimension_semantics=("arbitrary",), vmem_limit_bytes=VMEM_LIMIT_BYTES),
        name="post_mix_ffn",
    )(*ins_long, *ins_short, *wts)
    return outs[0:2], outs[2:4]


def _pad_rows(w, n):
    return jnp.pad(w, ((0, n - w.shape[0]), (0, 0)))


def _prepare_weights(w_in, a_mu, a_w0, a_w_w2, a_a0, a_w_a2, a_w_g2, a_k_k, a_k_a, a_r_k, a_gn_g,
                     a_gn_b, b_conv_w, b_a_log, b_dt_bias, b_norm_g, w_o, ln1_g, ln1_b, w_up,
                     f_conv_w, f_conv_b, w_down, ln2_g, ln2_b, w_ple, ple_g, w_ple_gate):
    lora = a_w_w2.shape[1]
    proj = (w_in[0].astype(BF16), _pad_rows(b_conv_w[0], SUBLANES))
    rwkv = (a_mu,
            _pad_rows(jnp.concatenate([a_w0, a_a0, a_k_k, a_k_a, a_r_k, a_gn_g, a_gn_b], axis=0), SUBLANES),
            _pad_rows(a_w_w2[0], LANES).astype(BF16),
            jnp.pad(a_w_a2[0], ((lora, LANES - 2 * lora), (0, 0))).astype(BF16),
            a_w_g2[0].astype(BF16))
    head_row = lambda vec: jnp.pad(vec, ((0, 0), (B_HEADS, LANES - 2 * B_HEADS)))
    gdn = (_pad_rows(jnp.concatenate([head_row(b_a_log), head_row(b_dt_bias), b_norm_g], axis=0), SUBLANES),)
    post = (w_o[0, :A_WIDTH].astype(BF16), w_o[0, A_WIDTH:].astype(BF16),
            _pad_rows(jnp.concatenate([ln1_g, ln1_b, ln2_g, ln2_b, ple_g], axis=0), SUBLANES),
            w_up[0].astype(BF16),
            _pad_rows(jnp.concatenate([f_conv_w[0], f_conv_b], axis=0), SUBLANES),
            w_down[0].astype(BF16), w_ple[0].astype(BF16), w_ple_gate[0].astype(BF16))
    return proj, rwkv, gdn, post


def _mix_tokens(x, p, states, weights, proj):
    _, rwkv_w, gdn_w, _ = weights
    a_wkv, a_shift, b_ssm, _, f_conv = (s[0] for s in states)
    bsz, seq, _ = x.shape
    m = bsz * seq
    if seq % SLOT_ROWS == 0:
        nb, log_c, slots, kind = 1, SLOT_ROWS.bit_length() - 1, bsz, 0
    else:
        assert seq == SUBLANES
        nb, log_c, slots, kind = SLOT_ROWS // seq, 3, m // SLOT_ROWS, 1
    proj_a, proj_b, proj_g, conv_new = proj
    ns, waves = MIXER_SLOTS[kind]
    o_a, o_b, wkv_new, shift_new, ssm_new = _mixers_call(
        proj_a.reshape(slots, -1, A_SHIFT_W), proj_b.reshape(slots, -1, B_MIX_W), proj_g.reshape(slots, -1, LANES),
        a_shift[:, None, :], a_wkv, b_ssm, rwkv_w, gdn_w, ns, waves, nb, log_c)
    post_ins = (o_a.reshape(m, A_WIDTH), o_b.reshape(m, B_WIDTH), x.reshape(m, D_MODEL), p.reshape(m, PLE_DIM), f_conv)
    return post_ins, (wkv_new, shift_new[:, 0, :], ssm_new, conv_new)


def kernel(x_prompt, x_sample, p_prompt, p_sample, state_a_wkv, state_a_shift, state_b_ssm, state_b_conv, state_ffn_conv, w_in, a_mu, a_w0, a_w_w2, a_a0, a_w_a2, a_w_g2, a_k_k, a_k_a, a_r_k, a_gn_g, a_gn_b, b_conv_w, b_a_log, b_dt_bias, b_norm_g, w_o, ln1_g, ln1_b, w_up, f_conv_w, f_conv_b, w_down, ln2_g, ln2_b, w_ple, ple_g, w_ple_gate):
    assert w_in.shape[0] == DEPTH
    weights = _prepare_weights(w_in, a_mu, a_w0, a_w_w2, a_a0, a_w_a2, a_w_g2, a_k_k, a_k_a, a_r_k,
                               a_gn_g, a_gn_b, b_conv_w, b_a_log, b_dt_bias, b_norm_g, w_o, ln1_g,
                               ln1_b, w_up, f_conv_w, f_conv_b, w_down, ln2_g, ln2_b, w_ple, ple_g,
                               w_ple_gate)
    bp = x_prompt.shape[0]
    zeros = lambda *s: jnp.zeros((DEPTH, bp) + s, x_prompt.dtype)
    prompt_init = (zeros(A_HEADS, A_HEAD, A_HEAD), zeros(A_SHIFT_W), zeros(B_HEADS, B_HEAD, B_HEAD),
                   zeros(B_CONV - 1, B_QKV), zeros(FFN_CONV - 1, D_FF))
    sample_init = (state_a_wkv, state_a_shift, state_b_ssm, state_b_conv, state_ffn_conv)
    rows = lambda x: x.reshape(-1, D_MODEL)
    prompt_proj, sample_proj = _proj_call(rows(x_prompt), prompt_init[3][0], x_prompt.shape[1],
                                          rows(x_sample), state_b_conv[0], x_sample.shape[1], *weights[0])
    prompt_post, prompt_states = _mix_tokens(x_prompt, p_prompt[0], prompt_init, weights, prompt_proj)
    sample_post, sample_states = _mix_tokens(x_sample, p_sample[0], sample_init, weights, sample_proj)
    (y_prompt, prompt_tails), (y_sample, sample_tails) = _post_call(
        prompt_post, x_prompt.shape[1], sample_post, x_sample.shape[1], weights[3])
    with_depth = lambda states: tuple(s[None] for s in states)
    return ((y_prompt.reshape(x_prompt.shape), y_sample.reshape(x_sample.shape))
            + with_depth(prompt_states + (prompt_tails,)) + with_depth(sample_states + (sample_tails,)))
```
